```python
import jax, jax.numpy as jnp
from jax import lax
import numpy as np

D_MODEL = 1024
BATCH = 4
SEQ = 8192
DEPTH = 4

N_MIXERS = 2
N_EVEN = (DEPTH + 1) // 2
N_ODD = DEPTH // 2
POOL_WINDOWS = (2, 4, 8, 16)
N_POOL_GROUPS = len(POOL_WINDOWS)
POOL_GROUP_DIM = D_MODEL // N_POOL_GROUPS
DIL_PATTERNS = ((128, 1), (512, 4), (2048, 16))
N_DIL_GROUPS = len(DIL_PATTERNS)
HEADS_PER_GROUP = 8
HEAD_DIM = D_MODEL // HEADS_PER_GROUP
GROUP_WIDTH = HEADS_PER_GROUP * HEAD_DIM
QKV_WIDTH = N_DIL_GROUPS * 3 * GROUP_WIDTH
ROPE_DIM = HEAD_DIM // 4
ROPE_THETA = 500000.0
D_FF = 2816
N_EXPERTS = 8
TOP_K = 2
D_FF_EXPERT = 3584
NORM_EPS = 1e-6

kernel_name = "hybrid_pool_dilated_attn_moe_adaln"


def _rmsnorm(t, g):
    t32 = t.astype(jnp.float32)
    y = t32 * lax.rsqrt(jnp.mean(t32 * t32, axis=-1, keepdims=True) + NORM_EPS)
    return (y * g.astype(jnp.float32)).astype(t.dtype)


def _rope_tables(positions):
    half = ROPE_DIM // 2
    inv_freq = ROPE_THETA ** (-jnp.arange(half, dtype=jnp.float32) * 2.0 / ROPE_DIM)
    ang = positions.astype(jnp.float32)[..., None] * inv_freq
    return jnp.cos(ang)[:, :, None, :], jnp.sin(ang)[:, :, None, :]


def _partial_rope(t, cos, sin):
    t32 = t.astype(jnp.float32)
    half = ROPE_DIM // 2
    x1, x2, rest = t32[..., :half], t32[..., half:ROPE_DIM], t32[..., ROPE_DIM:]
    out = jnp.concatenate([x1 * cos - x2 * sin, x2 * cos + x1 * sin, rest], axis=-1)
    return out.astype(t.dtype)


def _pool_mixer(h, w_pool, scale):
    B, S, D = h.shape
    hg = h.astype(jnp.float32).reshape(B, S, N_POOL_GROUPS, POOL_GROUP_DIM)
    cs = jnp.cumsum(hg, axis=1)
    t = jnp.arange(S)
    pooled = []
    for gi, w in enumerate(POOL_WINDOWS):
        c = cs[:, :, gi]
        lag = jnp.pad(c[:, :S - w], ((0, 0), (w, 0), (0, 0)))
        cnt = jnp.minimum(t + 1, w).astype(jnp.float32)[None, :, None]
        pooled.append((c - lag) / cnt)
    pooled = jnp.stack(pooled, axis=2) - hg
    y = jnp.einsum('bsgc,gcd->bsgd', pooled.astype(h.dtype), w_pool).reshape(B, S, D)
    return y * scale


def _dilated_window_attention(q, k, v, window, dilation):
    B, S, H, Dh = q.shape
    span = window // dilation
    L = S // dilation
    nb = -(-L // span)
    Lp = nb * span

    def to_strided_blocks(t):
        t = t.reshape(B, L, dilation, H, Dh).transpose(0, 2, 3, 1, 4)
        t = jnp.pad(t, ((0, 0), (0, 0), (0, 0), (0, Lp - L), (0, 0)))
        return t.reshape(B, dilation, H, nb, span, Dh)

    def with_prev_block(t):
        prev = jnp.pad(t[:, :, :, :-1], ((0, 0), (0, 0), (0, 0), (1, 0), (0, 0), (0, 0)))
        return jnp.concatenate([prev, t], axis=4)

    qb = to_strided_blocks(q)
    kk = with_prev_block(to_strided_blocks(k))
    vv = with_prev_block(to_strided_blocks(v))
    s = jnp.einsum('brhnqe,brhnke->brhnqk', qb, kk,
                   preferred_element_type=jnp.float32) * (HEAD_DIM ** -0.5)
    blk = jnp.arange(nb)[:, None, None]
    qpos = blk * span + jnp.arange(span)[None, :, None]
    kpos = (blk - 1) * span + jnp.arange(2 * span)[None, None, :]
    allowed = (kpos <= qpos) & (kpos >= qpos - span) & (kpos >= 0)
    s = jnp.where(allowed, s, -jnp.inf)
    m = jnp.max(s, axis=-1, keepdims=True)
    p = jnp.exp(s - m)
    den = jnp.sum(p, axis=-1, keepdims=True)
    o = jnp.einsum('brhnqk,brhnke->brhnqe', p, vv.astype(jnp.float32)) / den
    lse = (m + jnp.log(den))[..., 0]
    o = o.reshape(B, dilation, H, Lp, Dh)[:, :, :, :L].transpose(0, 3, 1, 2, 4).reshape(B, S, H, Dh)
    lse = lse.reshape(B, dilation, H, Lp)[..., :L].transpose(0, 3, 1, 2).reshape(B, S, H)
    return o, lse


def _dilated_attention_mixer(h, cos, sin, w_qkv, w_o, q_gain, k_gain):
    B, S, D = h.shape
    qkv = (h @ w_qkv).reshape(B, S, N_DIL_GROUPS, 3, HEADS_PER_GROUP, HEAD_DIM)
    outs, lses = [], []
    for g, (window, dilation) in enumerate(DIL_PATTERNS):
        q = _partial_rope(_rmsnorm(qkv[:, :, g, 0], q_gain[g]), cos, sin)
        k = _partial_rope(_rmsnorm(qkv[:, :, g, 1], k_gain[g]), cos, sin)
        v = qkv[:, :, g, 2]
        o, lse = _dilated_window_attention(q, k, v, window, dilation)
        outs.append(o)
        lses.append(lse)
    alpha = jax.nn.softmax(jnp.stack(lses, axis=0), axis=0)
    o = jnp.einsum('gbsh,gbshe->bshe', alpha, jnp.stack(outs, axis=0))
    return o.reshape(B, S, GROUP_WIDTH).astype(h.dtype) @ w_o


def _swiglu(h, w_gu, w_down):
    g, u = jnp.split(h @ w_gu, 2, axis=-1)
    return (jax.nn.silu(g) * u) @ w_down


def _moe_swiglu(h, w_router, b_router, w_gu, w_down):
    B, S, D = h.shape
    t = h.reshape(B * S, D)
    logits = (t @ w_router).astype(jnp.float32) + b_router.astype(jnp.float32)
    top_val, top_idx = lax.top_k(logits, TOP_K)
    gates = jax.nn.softmax(top_val, axis=-1)
    combine = jnp.sum(jax.nn.one_hot(top_idx, N_EXPERTS, dtype=jnp.float32) * gates[..., None], axis=1)
    y = jnp.zeros((B * S, D), jnp.float32)
    for e in range(N_EXPERTS):
        y = y + combine[:, e:e + 1] * _swiglu(t, w_gu[e], w_down[e]).astype(jnp.float32)
    return y.reshape(B, S, D).astype(h.dtype)


def setup_inputs(seed: int = 0) -> dict:
    key = jax.random.key(seed)
    ks = jax.random.split(key, 20)
    f32 = jnp.float32
    nrm = lambda k, shape, fan_in, gain=1.0: jax.random.normal(k, shape, f32) * (gain * fan_in ** -0.5)
    x = jax.random.normal(ks[0], (BATCH, SEQ, D_MODEL), f32)
    c = jax.random.normal(ks[1], (BATCH, D_MODEL), f32)
    offset = jax.random.randint(ks[2], (BATCH, 1), 0, 4096, dtype=jnp.int32)
    positions = offset + jnp.arange(SEQ, dtype=jnp.int32)[None, :]
    return {
        "x": x,
        "c": c,
        "positions": positions,
        "norm1_g": 1.0 + 0.05 * jax.random.normal(ks[3], (DEPTH, D_MODEL), f32),
        "norm2_g": 1.0 + 0.05 * jax.random.normal(ks[4], (DEPTH, D_MODEL), f32),
        "ada_w": nrm(ks[5], (DEPTH, D_MODEL, 6 * D_MODEL), D_MODEL, 0.5),
        "ada_b": 0.02 * jax.random.normal(ks[6], (DEPTH, 6 * D_MODEL), f32),
        "pool_w": nrm(ks[7], (N_EVEN, N_POOL_GROUPS, POOL_GROUP_DIM, POOL_GROUP_DIM), POOL_GROUP_DIM),
        "pool_scale": 1.0 + 0.1 * jax.random.normal(ks[8], (N_EVEN, D_MODEL), f32),
        "attn_w_qkv": nrm(ks[9], (N_ODD, D_MODEL, QKV_WIDTH), D_MODEL),
        "attn_w_o": nrm(ks[10], (N_ODD, GROUP_WIDTH, D_MODEL), GROUP_WIDTH),
        "q_norm_g": 1.0 + 0.05 * jax.random.normal(ks[11], (N_ODD, N_DIL_GROUPS, HEAD_DIM), f32),
        "k_norm_g": 1.0 + 0.05 * jax.random.normal(ks[12], (N_ODD, N_DIL_GROUPS, HEAD_DIM), f32),
        "ffn_w_gu": nrm(ks[13], (N_EVEN, D_MODEL, 2 * D_FF), D_MODEL),
        "ffn_w_down": nrm(ks[14], (N_EVEN, D_FF, D_MODEL), D_FF),
        "router_w": nrm(ks[15], (N_ODD, D_MODEL, N_EXPERTS), D_MODEL),
        "router_b": 0.01 * jax.random.normal(ks[16], (N_ODD, N_EXPERTS), f32),
        "moe_w_gu": nrm(ks[17], (N_ODD, N_EXPERTS, D_MODEL, 2 * D_FF_EXPERT), D_MODEL),
        "moe_w_down": nrm(ks[18], (N_ODD, N_EXPERTS, D_FF_EXPERT, D_MODEL), D_FF_EXPERT),
    }


def reference(x, c, positions, norm1_g, norm2_g, ada_w, ada_b, pool_w, pool_scale,
              attn_w_qkv, attn_w_o, q_norm_g, k_norm_g, ffn_w_gu, ffn_w_down,
              router_w, router_b, moe_w_gu, moe_w_down):
    cos, sin = _rope_tables(positions)
    c_act = jax.nn.silu(c)
    for i in range(DEPTH):
        j = i // 2
        mod = (c_act @ ada_w[i] + ada_b[i])[:, None, :]
        sh1, sc1, g1, sh2, sc2, g2 = jnp.split(mod, 6, axis=-1)
        h = _rmsnorm(x, norm1_g[i]) * (1.0 + sc1) + sh1
        if i % N_MIXERS == 0:
            y = _pool_mixer(h, pool_w[j], pool_scale[j])
        else:
            y = _dilated_attention_mixer(h, cos, sin, attn_w_qkv[j], attn_w_o[j],
                                         q_norm_g[j], k_norm_g[j])
        x = x + g1 * y
        h = _rmsnorm(x, norm2_g[i]) * (1.0 + sc2) + sh2
        if i % 2 == 0:
            y = _swiglu(h, ffn_w_gu[j], ffn_w_down[j])
        else:
            y = _moe_swiglu(h, router_w[j], router_b[j], moe_w_gu[j], moe_w_down[j])
        x = x + g2 * y
    return x
```

```python
import functools

import jax
import jax.numpy as jnp
from jax import lax
from jax.experimental import pallas as pl
from jax.experimental.pallas import tpu as pltpu

F32 = jnp.float32
BF16 = jnp.bfloat16
U32 = jnp.uint32
I32 = jnp.int32

POOL_WINDOWS = (2, 4, 8, 16)
POOL_HALO = 16
DIL_PATTERNS = ((128, 1), (512, 4), (2048, 16))
HEADS = 8
HEAD_DIM = 128
ROPE_DIM = 32
ROPE_THETA = 500000.0
N_EXPERTS = 8
NORM_EPS = 1e-6
NEG_BIG = -1e30

LANES = 128
SPAN = 128
MIB = 1024 * 1024
HIGHEST = lax.Precision.HIGHEST


def _cparams(n_axes, vmem_mib):
    return pltpu.CompilerParams(
        dimension_semantics=("arbitrary",) * n_axes,
        vmem_limit_bytes=int(vmem_mib * MIB),
    )


def _silu(v):
    return v / (1.0 + jnp.exp(-v))


def _norm_mod(t, gain, scale, shift):
    ms = jnp.mean(t * t, axis=-1, keepdims=True)
    return t * lax.rsqrt(ms + NORM_EPS) * gain * (1.0 + scale) + shift


def _resident(shape):
    nd = len(shape)
    return pl.BlockSpec(shape, lambda *_: (0,) * nd, pipeline_mode=pl.Buffered(1))


def _ada_kernel(c_ref, w_ref, b_ref, o_ref):
    ca = _silu(c_ref[...])
    o_ref[0] = jnp.dot(ca, w_ref[0], precision=HIGHEST, preferred_element_type=F32) + b_ref[0]


def _ada_modulation(c, ada_w, ada_b):
    depth, d, six_d = ada_w.shape
    b = c.shape[0]
    tn = 1536
    return pl.pallas_call(
        _ada_kernel,
        out_shape=jax.ShapeDtypeStruct((depth, b, six_d), F32),
        grid=(depth, six_d // tn),
        in_specs=[
            pl.BlockSpec((b, d), lambda l, j: (0, 0)),
            pl.BlockSpec((1, d, tn), lambda l, j: (l, 0, j)),
            pl.BlockSpec((1, 1, tn), lambda l, j: (l, 0, j)),
        ],
        out_specs=pl.BlockSpec((1, b, tn), lambda l, j: (l, 0, j)),
        compiler_params=_cparams(2, 32),
        name="ada_modulation",
    )(c, ada_w, ada_b.reshape(depth, 1, six_d))


def _rope_kernel(pos_ref, invf_ref, cs_ref, sn_ref):
    ang = pos_ref[...].astype(F32) * invf_ref[...]
    lane = lax.broadcasted_iota(I32, ang.shape, 1)
    half = ROPE_DIM // 2
    c = jnp.cos(ang)
    s = jnp.sin(ang)
    cs_ref[...] = jnp.where(lane < ROPE_DIM, c, 1.0)
    sn_ref[...] = jnp.where(lane < half, -s, jnp.where(lane < ROPE_DIM, s, 0.0))


def _rope_tables(positions):
    n = positions.size
    half = ROPE_DIM // 2
    inv_freq = ROPE_THETA ** (-jnp.arange(half, dtype=F32) * 2.0 / ROPE_DIM)
    invf = jnp.concatenate([inv_freq, inv_freq, jnp.zeros((LANES - ROPE_DIM,), F32)]).reshape(1, LANES)
    tm = min(1024, n)
    return pl.pallas_call(
        _rope_kernel,
        out_shape=(jax.ShapeDtypeStruct((n, LANES), F32),) * 2,
        grid=(n // tm,),
        in_specs=[pl.BlockSpec((tm, 1), lambda i: (i, 0)), pl.BlockSpec((1, LANES), lambda i: (0, 0))],
        out_specs=(pl.BlockSpec((tm, LANES), lambda i: (i, 0)),) * 2,
        compiler_params=_cparams(1, 32),
        name="rope_tables",
    )(positions.reshape(n, 1), invf)


def _pool_kernel(xc_ref, xh_ref, n1_ref, sh1_ref, sc1_ref, g1_ref, pw_ref, ps_ref,
                 n2_ref, sh2_ref, sc2_ref, x1_ref, h2_ref, *, ts):
    i = pl.program_id(1)
    xc = xc_ref[0]
    hc = _norm_mod(xc, n1_ref[...], sc1_ref[0], sh1_ref[0])
    hh = _norm_mod(xh_ref[0], n1_ref[...], sc1_ref[0], sh1_ref[0])
    hh = jnp.where(i > 0, hh, 0.0)
    hcat = jnp.concatenate([hh, hc], axis=0)
    pos = i * ts + lax.broadcasted_iota(I32, (ts, 1), 0)
    gd = hc.shape[1] // len(POOL_WINDOWS)
    ys = []
    for gi, w in enumerate(POOL_WINDOWS):
        s = hcat[:, gi * gd:(gi + 1) * gd]
        span = 1
        while span < w:
            s = s + pltpu.roll(s, span, 0)
            span *= 2
        win = s[POOL_HALO:, :]
        cnt = jnp.minimum(pos + 1, w).astype(F32)
        pooled = win / cnt - hc[:, gi * gd:(gi + 1) * gd]
        ys.append(jnp.dot(pooled.astype(BF16), pw_ref[gi], preferred_element_type=F32))
    y = jnp.concatenate(ys, axis=-1) * ps_ref[...]
    x1 = xc + g1_ref[0] * y
    x1_ref[0] = x1
    h2_ref[0] = _norm_mod(x1, n2_ref[...], sc2_ref[0], sh2_ref[0]).astype(BF16)


def _pool_layer(x3, mod, layer, n1g, n2g, pool_w, pool_scale):
    b, s, d = x3.shape
    ts = min(512, s)
    hb = ts // POOL_HALO
    ng, gd, _ = pool_w.shape

    def mspec(k):
        return pl.BlockSpec((1, 1, d), lambda bi, i: ((layer * b + bi) * 6 + k, 0, 0))

    vec = pl.BlockSpec((1, d), lambda bi, i: (0, 0))
    tile = pl.BlockSpec((1, ts, d), lambda bi, i: (bi, i, 0))
    return pl.pallas_call(
        functools.partial(_pool_kernel, ts=ts),
        out_shape=(jax.ShapeDtypeStruct((b, s, d), F32), jax.ShapeDtypeStruct((b, s, d), BF16)),
        grid=(b, s // ts),
        in_specs=[
            tile,
            pl.BlockSpec((1, POOL_HALO, d), lambda bi, i: (bi, jnp.maximum(i * hb - 1, 0), 0)),
            vec, mspec(0), mspec(1), mspec(2),
            pl.BlockSpec((ng, gd, gd), lambda bi, i: (0, 0, 0)),
            vec, vec, mspec(3), mspec(4),
        ],
        out_specs=(tile, tile),
        compiler_params=_cparams(2, 48),
        name="pool_mixer",
    )(x3, x3, n1g.reshape(1, d), mod, mod, mod, pool_w, pool_scale.reshape(1, d),
      n2g.reshape(1, d), mod, mod)


def _ffn_kernel(h_ref, x_ref, g2_ref, wgu_ref, wd_ref, o_ref, *, ff, fc):
    h = h_ref[...]
    acc = jnp.zeros(x_ref.shape, F32)
    for c in range(ff // fc):
        g = jnp.dot(h, wgu_ref[:, c * fc:(c + 1) * fc], preferred_element_type=F32)
        u = jnp.dot(h, wgu_ref[:, ff + c * fc:ff + (c + 1) * fc], preferred_element_type=F32)
        a = (_silu(g) * u).astype(BF16)
        acc = acc + jnp.dot(a, wd_ref[c * fc:(c + 1) * fc, :], preferred_element_type=F32)
    o_ref[...] = x_ref[...] + g2_ref[0] * acc


def _ffn_layer(h2, x1, mod, layer, batch, w_gu, w_down):
    n, d = x1.shape
    ff = w_down.shape[0]
    tm = min(512, n)
    per_b = (n // batch) // tm
    fc = ff // 2 if (ff // 2) % LANES == 0 else ff
    tile = lambda: pl.BlockSpec((tm, d), lambda i: (i, 0))
    return pl.pallas_call(
        functools.partial(_ffn_kernel, ff=ff, fc=fc),
        out_shape=jax.ShapeDtypeStruct((n, d), F32),
        grid=(n // tm,),
        in_specs=[
            tile(), tile(),
            pl.BlockSpec((1, 1, d), lambda i: ((layer * batch + i // per_b) * 6 + 5, 0, 0)),
            _resident((d, 2 * ff)), _resident((ff, d)),
        ],
        out_specs=tile(),
        compiler_params=_cparams(1, 56),
        name="dense_swiglu",
    )(h2, x1, mod, w_gu, w_down)


def _qkv_kernel(x_ref, n1_ref, sh1_ref, sc1_ref, w_ref, qg_ref, kg_ref, cs_ref, sn_ref, o_ref):
    h = _norm_mod(x_ref[...], n1_ref[...], sc1_ref[0], sh1_ref[0]).astype(BF16)
    cs = cs_ref[...]
    sn = sn_ref[...]
    first_half = lax.broadcasted_iota(I32, cs.shape, 1) < ROPE_DIM // 2
    gw = HEADS * HEAD_DIM
    n_groups = len(DIL_PATTERNS)
    for blk in range(3 * n_groups):
        g, part = divmod(blk, 3)
        y = jnp.dot(h, w_ref[:, blk * gw:(blk + 1) * gw], preferred_element_type=F32)
        if part == 2:
            o_ref[:, blk * gw:(blk + 1) * gw] = y.astype(BF16)
            continue
        gain = (qg_ref if part == 0 else kg_ref)[g:g + 1, :]
        for hd in range(HEADS):
            yh = y[:, hd * HEAD_DIM:(hd + 1) * HEAD_DIM]
            ms = jnp.mean(yh * yh, axis=-1, keepdims=True)
            yn = yh * lax.rsqrt(ms + NORM_EPS) * gain
            partner = jnp.where(first_half, pltpu.roll(yn, LANES - ROPE_DIM // 2, 1),
                                pltpu.roll(yn, ROPE_DIM // 2, 1))
            c0 = blk * gw + hd * HEAD_DIM
            o_ref[:, c0:c0 + HEAD_DIM] = (yn * cs + partner * sn).astype(BF16)


def _qkv_layer(x, mod, layer, batch, n1g, w_qkv, qg, kg, cs, sn):
    n, d = x.shape
    width = w_qkv.shape[1]
    tm = min(512, n)
    per_b = (n // batch) // tm

    def mspec(k):
        return pl.BlockSpec((1, 1, d), lambda i: ((layer * batch + i // per_b) * 6 + k, 0, 0))

    small = lambda a: pl.BlockSpec(a.shape, lambda i: (0, 0))
    return pl.pallas_call(
        _qkv_kernel,
        out_shape=jax.ShapeDtypeStruct((n, width), BF16),
        grid=(n // tm,),
        in_specs=[
            pl.BlockSpec((tm, d), lambda i: (i, 0)),
            pl.BlockSpec((1, d), lambda i: (0, 0)), mspec(0), mspec(1),
            _resident((d, width)), small(qg), small(kg),
            pl.BlockSpec((tm, LANES), lambda i: (i, 0)),
            pl.BlockSpec((tm, LANES), lambda i: (i, 0)),
        ],
        out_specs=pl.BlockSpec((tm, width), lambda i: (i, 0)),
        compiler_params=_cparams(1, 58),
        name="qkv_proj",
    )(x, n1g.reshape(1, d), mod, mod, w_qkv, qg, kg, cs, sn)


def _attn_kernel(q_ref, kc_ref, vc_ref, kh_ref, vh_ref, o_ref, l_ref, kx_ref, vx_ref, *, qb_rows):
    n = pl.program_id(2)
    nq = qb_rows // SPAN
    kx_ref[0:SPAN, :] = kh_ref[0]
    kx_ref[SPAN:, :] = kc_ref[0]
    vx_ref[0:SPAN, :] = vh_ref[0]
    vx_ref[SPAN:, :] = vc_ref[0]
    row = lax.broadcasted_iota(I32, (SPAN, 2 * SPAN), 0)
    col = lax.broadcasted_iota(I32, (SPAN, 2 * SPAN), 1)
    band = (col >= row) & (col <= row + SPAN)
    lane = lax.broadcasted_iota(I32, (SPAN, LANES), 1)
    scale = HEAD_DIM ** -0.5

    def body(qb, carry):
        r0 = pl.multiple_of(qb * SPAN, SPAN)
        has_prev = (n * nq + qb) > 0
        mask = band & ((col >= SPAN) | has_prev)
        lse_tile = jnp.zeros((SPAN, LANES), F32)
        for hd in range(HEADS):
            cols = slice(hd * HEAD_DIM, (hd + 1) * HEAD_DIM)
            q = q_ref[0, pl.ds(r0, SPAN), cols]
            k = kx_ref[pl.ds(r0, 2 * SPAN), cols]
            v = vx_ref[pl.ds(r0, 2 * SPAN), cols]
            s = lax.dot_general(q, k, (((1,), (1,)), ((), ())), preferred_element_type=F32) * scale
            s = jnp.where(mask, s, NEG_BIG)
            m = jnp.max(s, axis=-1, keepdims=True)
            p = jnp.exp(s - m)
            den = jnp.sum(p, axis=-1, keepdims=True)
            o = jnp.dot(p.astype(BF16), v, preferred_element_type=F32) / den
            o_ref[0, pl.ds(r0, SPAN), cols] = o.astype(BF16)
            lse_tile = jnp.where(lane == hd, m + jnp.log(den), lse_tile)
        l_ref[0, pl.ds(r0, SPAN), :] = lse_tile
        return carry

    lax.fori_loop(0, nq, body, 0)


def _attn_group(qkv, batch, group, dilation):
    n, width = qkv.shape
    s = n // batch
    l = s // dilation
    gw = HEADS * HEAD_DIM
    nblk = width // gw
    qb_rows = min(512, l)
    sub = qb_rows // SPAN
    qkv_v = qkv.reshape(batch, l, dilation * width)

    def cur(part):
        return pl.BlockSpec((1, qb_rows, gw), lambda b, r, i: (b, i, r * nblk + group * 3 + part))

    def halo(part):
        return pl.BlockSpec((1, SPAN, gw),
                            lambda b, r, i: (b, jnp.maximum(i * sub - 1, 0), r * nblk + group * 3 + part))

    o, lse = pl.pallas_call(
        functools.partial(_attn_kernel, qb_rows=qb_rows),
        out_shape=(jax.ShapeDtypeStruct((batch, l, dilation * gw), BF16),
                   jax.ShapeDtypeStruct((batch, l, dilation * LANES), F32)),
        grid=(batch, dilation, l // qb_rows),
        in_specs=[cur(0), cur(1), cur(2), halo(1), halo(2)],
        out_specs=(pl.BlockSpec((1, qb_rows, gw), lambda b, r, i: (b, i, r)),
                   pl.BlockSpec((1, qb_rows, LANES), lambda b, r, i: (b, i, r))),
        scratch_shapes=[pltpu.VMEM((qb_rows + SPAN, gw), BF16), pltpu.VMEM((qb_rows + SPAN, gw), BF16)],
        compiler_params=_cparams(3, 40),
        name=f"dilated_attn_g{group}",
    )(qkv_v, qkv_v, qkv_v, qkv_v, qkv_v)
    return o.reshape(n, gw), lse.reshape(n, LANES)


def _merge_kernel(o0_ref, o1_ref, o2_ref, l0_ref, l1_ref, l2_ref, x_ref, wo_ref, g1_ref,
                  n2_ref, sh2_ref, sc2_ref, wr_ref, rb_ref, x1_ref, hp_ref, lg_ref):
    l0, l1, l2 = l0_ref[...], l1_ref[...], l2_ref[...]
    m = jnp.maximum(jnp.maximum(l0, l1), l2)
    e0, e1, e2 = jnp.exp(l0 - m), jnp.exp(l1 - m), jnp.exp(l2 - m)
    den = e0 + e1 + e2
    a0, a1, a2 = e0 / den, e1 / den, e2 / den
    parts = []
    for hd in range(HEADS):
        cols = slice(hd * HEAD_DIM, (hd + 1) * HEAD_DIM)
        parts.append(a0[:, hd:hd + 1] * o0_ref[:, cols].astype(F32)
                     + a1[:, hd:hd + 1] * o1_ref[:, cols].astype(F32)
                     + a2[:, hd:hd + 1] * o2_ref[:, cols].astype(F32))
    o = jnp.concatenate(parts, axis=-1).astype(BF16)
    y = jnp.dot(o, wo_ref[...], preferred_element_type=F32)
    x1 = x_ref[...] + g1_ref[0] * y
    x1_ref[...] = x1
    h2 = _norm_mod(x1, n2_ref[...], sc2_ref[0], sh2_ref[0])
    lg_ref[...] = jnp.dot(h2, wr_ref[...], precision=HIGHEST, preferred_element_type=F32) + rb_ref[...]
    half = h2.shape[1] // 2
    lo = lax.bitcast_convert_type(h2[:, :half].astype(BF16).astype(F32), U32)
    hi = lax.bitcast_convert_type(h2[:, half:].astype(BF16).astype(F32), U32)
    hp_ref[...] = (lo >> 16) | (hi & jnp.uint32(0xFFFF0000))


def _merge_layer(os_, lses, x, mod, layer, batch, w_o, n2g, w_router, b_router):
    n, d = x.shape
    tm = min(512, n)
    per_b = (n // batch) // tm
    ne = w_router.shape[1]
    wr = jnp.zeros((d, LANES), F32).at[:, :ne].set(w_router)
    rb = jnp.zeros((1, LANES), F32).at[0, :ne].set(b_router)

    def mspec(k):
        return pl.BlockSpec((1, 1, d), lambda i: ((layer * batch + i // per_b) * 6 + k, 0, 0))

    tile = lambda w: pl.BlockSpec((tm, w), lambda i: (i, 0))
    vec = lambda w: pl.BlockSpec((1, w), lambda i: (0, 0))
    return pl.pallas_call(
        _merge_kernel,
        out_shape=(jax.ShapeDtypeStruct((n, d), F32), jax.ShapeDtypeStruct((n, d // 2), U32),
                   jax.ShapeDtypeStruct((n, LANES), F32)),
        grid=(n // tm,),
        in_specs=[tile(d), tile(d), tile(d), tile(LANES), tile(LANES), tile(LANES), tile(d),
                  _resident((d, d)), mspec(2), vec(d), mspec(3), mspec(4),
                  _resident((d, LANES)), vec(LANES)],
        out_specs=(tile(d), tile(d // 2), tile(LANES)),
        compiler_params=_cparams(1, 48),
        name="attn_merge_proj",
    )(*os_, *lses, x, w_o, mod, n2g.reshape(1, d), mod, mod, wr, rb)


def _route_kernel(lg_ref, route_ref, gate_ref, cnt_ref, carry_ref, *, tr):
    @pl.when(pl.program_id(0) == 0)
    def _():
        carry_ref[...] = jnp.zeros_like(carry_ref)

    lane = lax.broadcasted_iota(I32, (tr, LANES), 1)
    lg = jnp.where(lane < N_EXPERTS, lg_ref[...], NEG_BIG)
    m1 = jnp.max(lg, axis=-1, keepdims=True)
    i1 = jnp.min(jnp.where(lg == m1, lane, LANES), axis=-1, keepdims=True)
    lg2 = jnp.where(lane == i1, NEG_BIG, lg)
    m2 = jnp.max(lg2, axis=-1, keepdims=True)
    i2 = jnp.min(jnp.where(lg2 == m2, lane, LANES), axis=-1, keepdims=True)
    e = jnp.exp(m2 - m1)
    gate_ref[...] = jnp.where(lane == 0, 1.0 / (1.0 + e), jnp.where(lane == 1, e / (1.0 + e), 0.0))

    sel1 = lane == i1
    sel2 = lane == i2
    chosen = jnp.where(sel1 | sel2, 1.0, 0.0)
    r = lax.broadcasted_iota(I32, (tr, tr), 0)
    c = lax.broadcasted_iota(I32, (tr, tr), 1)
    earlier = jnp.where(c < r, 1.0, 0.0).astype(BF16)
    before = jnp.dot(earlier, chosen.astype(BF16), preferred_element_type=F32) + carry_ref[...]
    rank1 = jnp.sum(jnp.where(sel1, before, 0.0), axis=-1, keepdims=True).astype(I32)
    rank2 = jnp.sum(jnp.where(sel2, before, 0.0), axis=-1, keepdims=True).astype(I32)
    route_ref[...] = jnp.where(lane == 0, i1, jnp.where(lane == 1, i2,
                               jnp.where(lane == 2, rank1, jnp.where(lane == 3, rank2, 0))))
    total = carry_ref[...] + jnp.sum(chosen, axis=0, keepdims=True)
    carry_ref[...] = total
    cnt_ref[...] = total.astype(I32)


def _route(logits):
    n = logits.shape[0]
    tr = min(512, n)
    tile = pl.BlockSpec((tr, LANES), lambda i: (i, 0))
    return pl.pallas_call(
        functools.partial(_route_kernel, tr=tr),
        out_shape=(jax.ShapeDtypeStruct((n, LANES), I32), jax.ShapeDtypeStruct((n, LANES), F32),
                   jax.ShapeDtypeStruct((1, LANES), I32)),
        grid=(n // tr,),
        in_specs=[tile],
        out_specs=(tile, tile, pl.BlockSpec((1, LANES), lambda i: (0, 0))),
        scratch_shapes=[pltpu.VMEM((1, LANES), F32)],
        compiler_params=_cparams(1, 32),
        name="moe_route",
    )(logits)


def _dispatch_kernel(offs_ref, rt_ref, h_ref, xs_in_ref, xs_ref, sem, *, tm):
    del xs_in_ref

    def row_copy(t, slot):
        dst = offs_ref[rt_ref[0, slot, t]] + rt_ref[0, 2 + slot, t]
        return pltpu.make_async_copy(h_ref.at[pl.ds(t, 1)], xs_ref.at[pl.ds(dst, 1)], sem)

    def start(t, carry):
        row_copy(t, 0).start()
        row_copy(t, 1).start()
        return carry

    def wait(t, carry):
        row_copy(t, 0).wait()
        row_copy(t, 1).wait()
        return carry

    lax.fori_loop(0, tm, start, 0)
    lax.fori_loop(0, tm, wait, 0)


def _dispatch(h2p, route_t, offs, rows_padded, tm):
    n, hw = h2p.shape
    xs0 = jnp.zeros((rows_padded, hw), U32)
    grid_spec = pltpu.PrefetchScalarGridSpec(
        num_scalar_prefetch=1,
        grid=(n // tm,),
        in_specs=[
            pl.BlockSpec((1, 4, tm), lambda i, offs: (i, 0, 0), memory_space=pltpu.SMEM),
            pl.BlockSpec((tm, hw), lambda i, offs: (i, 0)),
            pl.BlockSpec(memory_space=pl.ANY),
        ],
        out_specs=pl.BlockSpec(memory_space=pl.ANY),
        scratch_shapes=[pltpu.SemaphoreType.DMA(())],
    )
    return pl.pallas_call(
        functools.partial(_dispatch_kernel, tm=tm),
        out_shape=jax.ShapeDtypeStruct((rows_padded, hw), U32),
        grid_spec=grid_spec,
        input_output_aliases={3: 0},
        compiler_params=_cparams(1, 32),
        name="moe_dispatch",
    )(offs, route_t, h2p, xs0)


def _expert_kernel(te_ref, tv_ref, xs_ref, wg_ref, wu_ref, wd_ref, ys_ref, xb_ref, acc_ref, *, nfc):
    del te_ref
    i = pl.program_id(0)
    j = pl.program_id(1)

    @pl.when((tv_ref[i] == 0) & (j == nfc - 1))
    def _():
        ys_ref[...] = jnp.zeros_like(ys_ref)

    @pl.when(tv_ref[i] == 1)
    def _():
        @pl.when(j == 0)
        def _():
            w = xs_ref[...]
            half = w.shape[1]
            xb_ref[:, :half] = lax.bitcast_convert_type(w << 16, F32).astype(BF16)
            xb_ref[:, half:] = lax.bitcast_convert_type(w & jnp.uint32(0xFFFF0000), F32).astype(BF16)
            acc_ref[...] = jnp.zeros_like(acc_ref)

        xb = xb_ref[...]
        g = jnp.dot(xb, wg_ref[0], preferred_element_type=F32)
        u = jnp.dot(xb, wu_ref[0], preferred_element_type=F32)
        a = (_silu(g) * u).astype(BF16)
        acc_ref[...] += jnp.dot(a, wd_ref[0], preferred_element_type=F32)

        @pl.when(j == nfc - 1)
        def _():
            ys_ref[...] = acc_ref[...]


def _expert_ffn(xs, tile_expert, tile_valid, w_gu, w_down, tm):
    rows, hw = xs.shape
    ne, d, two_ff = w_gu.shape
    ff = two_ff // 2
    fc = 896 if ff % 896 == 0 else ff
    nfc = ff // fc
    n_tiles = rows // tm

    def chunk(i, j, tv):
        return jnp.where(tv[i] == 1, j, nfc - 1)

    grid_spec = pltpu.PrefetchScalarGridSpec(
        num_scalar_prefetch=2,
        grid=(n_tiles, nfc),
        in_specs=[
            pl.BlockSpec((tm, hw), lambda i, j, te, tv: (i, 0)),
            pl.BlockSpec((1, d, fc), lambda i, j, te, tv: (te[i], 0, chunk(i, j, tv))),
            pl.BlockSpec((1, d, fc), lambda i, j, te, tv: (te[i], 0, nfc + chunk(i, j, tv))),
            pl.BlockSpec((1, fc, d), lambda i, j, te, tv: (te[i], chunk(i, j, tv), 0)),
        ],
        out_specs=pl.BlockSpec((tm, d), lambda i, j, te, tv: (i, 0)),
        scratch_shapes=[pltpu.VMEM((tm, d), BF16), pltpu.VMEM((tm, d), F32)],
    )
    return pl.pallas_call(
        functools.partial(_expert_kernel, nfc=nfc),
        out_shape=jax.ShapeDtypeStruct((rows, d), F32),
        grid_spec=grid_spec,
        compiler_params=_cparams(2, 48),
        name="moe_expert_swiglu",
    )(tile_expert, tile_valid, xs, w_gu, w_gu, w_down)


def _combine_kernel(offs_ref, rt_ref, gate_ref, x_ref, g2_ref, ys_ref, o_ref, ybuf, sem, *, tm):
    def row_copy(t, slot):
        src = offs_ref[rt_ref[0, slot, t]] + rt_ref[0, 2 + slot, t]
        return pltpu.make_async_copy(ys_ref.at[pl.ds(src, 1)], ybuf.at[slot, pl.ds(t, 1)], sem)

    def start(t, carry):
        row_copy(t, 0).start()
        row_copy(t, 1).start()
        return carry

    def wait(t, carry):
        row_copy(t, 0).wait()
        row_copy(t, 1).wait()
        return carry

    lax.fori_loop(0, tm, start, 0)
    lax.fori_loop(0, tm, wait, 0)
    gate = gate_ref[...]
    y = gate[:, 0:1] * ybuf[0] + gate[:, 1:2] * ybuf[1]
    o_ref[...] = x_ref[...] + g2_ref[0] * y


def _combine(ys, route_t, gates, offs, x1, mod, layer, batch, tm):
    n, d = x1.shape
    per_b = (n // batch) // tm
    grid_spec = pltpu.PrefetchScalarGridSpec(
        num_scalar_prefetch=1,
        grid=(n // tm,),
        in_specs=[
            pl.BlockSpec((1, 4, tm), lambda i, offs: (i, 0, 0), memory_space=pltpu.SMEM),
            pl.BlockSpec((tm, LANES), lambda i, offs: (i, 0)),
            pl.BlockSpec((tm, d), lambda i, offs: (i, 0)),
            pl.BlockSpec((1, 1, d), lambda i, offs: ((layer * batch + i // per_b) * 6 + 5, 0, 0)),
            pl.BlockSpec(memory_space=pl.ANY),
        ],
        out_specs=pl.BlockSpec((tm, d), lambda i, offs: (i, 0)),
        scratch_shapes=[pltpu.VMEM((2, tm, d), F32), pltpu.SemaphoreType.DMA(())],
    )
    return pl.pallas_call(
        functools.partial(_combine_kernel, tm=tm),
        out_shape=jax.ShapeDtypeStruct((n, d), F32),
        grid_spec=grid_spec,
        compiler_params=_cparams(1, 32),
        name="moe_combine",
    )(offs, route_t, gates, x1, mod, ys)


def _moe_layer(h2p, logits, x1, mod, layer, batch, w_gu, w_down):
    n = x1.shape[0]
    tm = min(512, n)
    route, gates, counts = _route(logits)
    counts = counts[0, :N_EXPERTS]
    padded = ((counts + tm - 1) // tm) * tm
    ends = jnp.cumsum(padded)
    offs = (ends - padded).astype(I32)
    n_tiles = (2 * n) // tm + N_EXPERTS
    rows_padded = n_tiles * tm
    tile_start = jnp.arange(n_tiles, dtype=I32) * tm
    tile_valid = (tile_start < ends[-1]).astype(I32)
    tile_expert = jnp.minimum(jnp.sum(tile_start[:, None] >= ends[None, :], axis=1), N_EXPERTS - 1).astype(I32)
    route_t = route[:, :4].reshape(n // tm, tm, 4).transpose(0, 2, 1)
    xs = _dispatch(h2p, route_t, offs, rows_padded, tm)
    ys = _expert_ffn(xs, tile_expert, tile_valid, w_gu, w_down, tm)
    return _combine(ys, route_t, gates, offs, x1, mod, layer, batch, tm)


def kernel(x, c, positions, norm1_g, norm2_g, ada_w, ada_b, pool_w, pool_scale, attn_w_qkv, attn_w_o,
           q_norm_g, k_norm_g, ffn_w_gu, ffn_w_down, router_w, router_b, moe_w_gu, moe_w_down):
    batch, seq, d = x.shape
    depth = ada_w.shape[0]
    n = batch * seq
    mod = _ada_modulation(c, ada_w, ada_b).reshape(depth * batch * 6, 1, d)
    cs, sn = _rope_tables(positions)
    xf = x.reshape(n, d)
    for layer in range(depth):
        j = layer // 2
        if layer % 2 == 0:
            x1, h2 = _pool_layer(xf.reshape(batch, seq, d), mod, layer, norm1_g[layer], norm2_g[layer],
                                 pool_w[j].astype(BF16), pool_scale[j])
            xf = _ffn_layer(h2.reshape(n, d), x1.reshape(n, d), mod, layer, batch,
                            ffn_w_gu[j].astype(BF16), ffn_w_down[j].astype(BF16))
        else:
            qkv = _qkv_layer(xf, mod, layer, batch, norm1_g[layer], attn_w_qkv[j].astype(BF16),
                             q_norm_g[j], k_norm_g[j], cs, sn)
            outs = [_attn_group(qkv, batch, g, dil) for g, (_, dil) in enumerate(DIL_PATTERNS)]
            x1, h2p, logits = _merge_layer([o for o, _ in outs], [l for _, l in outs], xf, mod, layer, batch,
                                           attn_w_o[j].astype(BF16), norm2_g[layer], router_w[j], router_b[j])
            xf = _moe_layer(h2p, logits, x1, mod, layer, batch,
                            moe_w_gu[j].astype(BF16), moe_w_down[j].astype(BF16))
    return xf.reshape(batch, seq, d)
```

```python
import functools

import jax
import jax.numpy as jnp
from jax import lax
from jax.experimental import pallas as pl
from jax.experimental.pallas import tpu as pltpu

F32 = jnp.float32
BF16 = jnp.bfloat16
U32 = jnp.uint32
I32 = jnp.int32

POOL_WINDOWS = (2, 4, 8, 16)
POOL_HALO = 16
DIL_PATTERNS = ((128, 1), (512, 4), (2048, 16))
HEADS = 8
HEAD_DIM = 128
ROPE_DIM = 32
ROPE_THETA = 500000.0
N_EXPERTS = 8
NORM_EPS = 1e-6
NEG_BIG = -1e30

LANES = 128
SPAN = 128
MIB = 1024 * 1024
HIGHEST = lax.Precision.HIGHEST


def _cparams(n_axes, vmem_mib):
    return pltpu.CompilerParams(
        dimension_semantics=("arbitrary",) * n_axes,
        vmem_limit_bytes=int(vmem_mib * MIB),
    )


def _silu(v):
    return v / (1.0 + jnp.exp(-v))


def _norm_mod(t, gain, scale, shift):
    ms = jnp.mean(t * t, axis=-1, keepdims=True)
    return t * lax.rsqrt(ms + NORM_EPS) * gain * (1.0 + scale) + shift


def _resident(shape):
    nd = len(shape)
    return pl.BlockSpec(shape, lambda *_: (0,) * nd, pipeline_mode=pl.Buffered(1))


def _ada_kernel(c_ref, w_ref, b_ref, o_ref):
    ca = _silu(c_ref[...])
    o_ref[0] = jnp.dot(ca, w_ref[0], precision=HIGHEST, preferred_element_type=F32) + b_ref[0]


def _ada_modulation(c, ada_w, ada_b):
    depth, d, six_d = ada_w.shape
    b = c.shape[0]
    tn = 1536
    return pl.pallas_call(
        _ada_kernel,
        out_shape=jax.ShapeDtypeStruct((depth, b, six_d), F32),
        grid=(depth, six_d // tn),
        in_specs=[
            pl.BlockSpec((b, d), lambda l, j: (0, 0)),
            pl.BlockSpec((1, d, tn), lambda l, j: (l, 0, j)),
            pl.BlockSpec((1, 1, tn), lambda l, j: (l, 0, j)),
        ],
        out_specs=pl.BlockSpec((1, b, tn), lambda l, j: (l, 0, j)),
        compiler_params=_cparams(2, 32),
        name="ada_modulation",
    )(c, ada_w, ada_b.reshape(depth, 1, six_d))


def _rope_kernel(pos_ref, invf_ref, cs_ref, sn_ref):
    ang = pos_ref[...].astype(F32) * invf_ref[...]
    lane = lax.broadcasted_iota(I32, ang.shape, 1)
    half = ROPE_DIM // 2
    c = jnp.cos(ang)
    s = jnp.sin(ang)
    cs_ref[...] = jnp.where(lane < ROPE_DIM, c, 1.0)
    sn_ref[...] = jnp.where(lane < half, -s, jnp.where(lane < ROPE_DIM, s, 0.0))


def _rope_tables(positions):
    n = positions.size
    half = ROPE_DIM // 2
    inv_freq = ROPE_THETA ** (-jnp.arange(half, dtype=F32) * 2.0 / ROPE_DIM)
    invf = jnp.concatenate([inv_freq, inv_freq, jnp.zeros((LANES - ROPE_DIM,), F32)]).reshape(1, LANES)
    tm = min(1024, n)
    return pl.pallas_call(
        _rope_kernel,
        out_shape=(jax.ShapeDtypeStruct((n, LANES), F32),) * 2,
        grid=(n // tm,),
        in_specs=[pl.BlockSpec((tm, 1), lambda i: (i, 0)), pl.BlockSpec((1, LANES), lambda i: (0, 0))],
        out_specs=(pl.BlockSpec((tm, LANES), lambda i: (i, 0)),) * 2,
        compiler_params=_cparams(1, 32),
        name="rope_tables",
    )(positions.reshape(n, 1), invf)


def _pool_kernel(xc_ref, xh_ref, n1_ref, sh1_ref, sc1_ref, g1_ref, pw_ref, ps_ref,
                 n2_ref, sh2_ref, sc2_ref, x1_ref, h2_ref, *, ts):
    i = pl.program_id(1)
    xc = xc_ref[0]
    hc = _norm_mod(xc, n1_ref[...], sc1_ref[0], sh1_ref[0])
    hh = _norm_mod(xh_ref[0], n1_ref[...], sc1_ref[0], sh1_ref[0])
    hh = jnp.where(i > 0, hh, 0.0)
    hcat = jnp.concatenate([hh, hc], axis=0)
    pos = i * ts + lax.broadcasted_iota(I32, (ts, 1), 0)
    gd = hc.shape[1] // len(POOL_WINDOWS)
    ys = []
    for gi, w in enumerate(POOL_WINDOWS):
        s = hcat[:, gi * gd:(gi + 1) * gd]
        span = 1
        while span < w:
            s = s + pltpu.roll(s, span, 0)
            span *= 2
        win = s[POOL_HALO:, :]
        cnt = jnp.minimum(pos + 1, w).astype(F32)
        pooled = win / cnt - hc[:, gi * gd:(gi + 1) * gd]
        ys.append(jnp.dot(pooled.astype(BF16), pw_ref[gi], preferred_element_type=F32))
    y = jnp.concatenate(ys, axis=-1) * ps_ref[...]
    x1 = xc + g1_ref[0] * y
    x1_ref[0] = x1
    h2_ref[0] = _norm_mod(x1, n2_ref[...], sc2_ref[0], sh2_ref[0]).astype(BF16)


def _pool_layer(x3, mod, layer, n1g, n2g, pool_w, pool_scale):
    b, s, d = x3.shape
    ts = min(512, s)
    hb = ts // POOL_HALO
    ng, gd, _ = pool_w.shape

    def mspec(k):
        return pl.BlockSpec((1, 1, d), lambda bi, i: ((layer * b + bi) * 6 + k, 0, 0))

    vec = pl.BlockSpec((1, d), lambda bi, i: (0, 0))
    tile = pl.BlockSpec((1, ts, d), lambda bi, i: (bi, i, 0))
    return pl.pallas_call(
        functools.partial(_pool_kernel, ts=ts),
        out_shape=(jax.ShapeDtypeStruct((b, s, d), F32), jax.ShapeDtypeStruct((b, s, d), BF16)),
        grid=(b, s // ts),
        in_specs=[
            tile,
            pl.BlockSpec((1, POOL_HALO, d), lambda bi, i: (bi, jnp.maximum(i * hb - 1, 0), 0)),
            vec, mspec(0), mspec(1), mspec(2),
            pl.BlockSpec((ng, gd, gd), lambda bi, i: (0, 0, 0)),
            vec, vec, mspec(3), mspec(4),
        ],
        out_specs=(tile, tile),
        compiler_params=_cparams(2, 48),
        name="pool_mixer",
    )(x3, x3, n1g.reshape(1, d), mod, mod, mod, pool_w, pool_scale.reshape(1, d),
      n2g.reshape(1, d), mod, mod)


def _ffn_kernel(h_ref, x_ref, g2_ref, wgu_ref, wd_ref, o_ref, *, ff, fc):
    h = h_ref[...]
    acc = jnp.zeros(x_ref.shape, F32)
    for c in range(ff // fc):
        g = jnp.dot(h, wgu_ref[:, c * fc:(c + 1) * fc], preferred_element_type=F32)
        u = jnp.dot(h, wgu_ref[:, ff + c * fc:ff + (c + 1) * fc], preferred_element_type=F32)
        a = (_silu(g) * u).astype(BF16)
        acc = acc + jnp.dot(a, wd_ref[c * fc:(c + 1) * fc, :], preferred_element_type=F32)
    o_ref[...] = x_ref[...] + g2_ref[0] * acc


def _ffn_layer(h2, x1, mod, layer, batch, w_gu, w_down):
    n, d = x1.shape
    ff = w_down.shape[0]
    tm = min(512, n)
    per_b = (n // batch) // tm
    fc = ff // 2 if (ff // 2) % LANES == 0 else ff
    tile = lambda: pl.BlockSpec((tm, d), lambda i: (i, 0))
    return pl.pallas_call(
        functools.partial(_ffn_kernel, ff=ff, fc=fc),
        out_shape=jax.ShapeDtypeStruct((n, d), F32),
        grid=(n // tm,),
        in_specs=[
            tile(), tile(),
            pl.BlockSpec((1, 1, d), lambda i: ((layer * batch + i // per_b) * 6 + 5, 0, 0)),
            _resident((d, 2 * ff)), _resident((ff, d)),
        ],
        out_specs=tile(),
        compiler_params=_cparams(1, 56),
        name="dense_swiglu",
    )(h2, x1, mod, w_gu, w_down)


def _residue_major_perm(tm, dilation):
    per = tm // dilation
    p = jnp.arange(tm)
    src = (p % per) * dilation + p // per
    return (src[:, None] == jnp.arange(tm)[None, :]).astype(F32)


def _qkv_kernel(x_ref, n1_ref, sh1_ref, sc1_ref, w_ref, qg_ref, kg_ref, cs_ref, sn_ref, *rest, dils):
    n_perm = sum(1 for dl in dils if dl > 1)
    perm_refs = list(rest[:n_perm])
    out_refs = rest[n_perm:n_perm + len(dils)]
    stage_ref = rest[-1]
    h = _norm_mod(x_ref[0], n1_ref[...], sc1_ref[0], sh1_ref[0]).astype(BF16)
    tm = h.shape[0]
    cs = cs_ref[...]
    sn = sn_ref[...]
    first_half = lax.broadcasted_iota(I32, cs.shape, 1) < ROPE_DIM // 2
    gw = HEADS * HEAD_DIM
    for g, dl in enumerate(dils):
        dst = out_refs[g].at[0, 0] if dl == 1 else stage_ref
        for part in range(3):
            blk = g * 3 + part
            y = jnp.dot(h, w_ref[:, blk * gw:(blk + 1) * gw], preferred_element_type=F32)
            if part == 2:
                dst[:, part * gw:(part + 1) * gw] = y.astype(BF16)
                continue
            gain = (qg_ref if part == 0 else kg_ref)[g:g + 1, :]
            for hd in range(HEADS):
                yh = y[:, hd * HEAD_DIM:(hd + 1) * HEAD_DIM]
                ms = jnp.mean(yh * yh, axis=-1, keepdims=True)
                yn = yh * lax.rsqrt(ms + NORM_EPS) * gain
                partner = jnp.where(first_half, pltpu.roll(yn, LANES - ROPE_DIM // 2, 1),
                                    pltpu.roll(yn, ROPE_DIM // 2, 1))
                c0 = part * gw + hd * HEAD_DIM
                dst[:, c0:c0 + HEAD_DIM] = (yn * cs + partner * sn).astype(BF16)
        if dl > 1:
            perm = perm_refs.pop(0)[...]
            for part in range(3):
                cols = slice(part * gw, (part + 1) * gw)
                shuffled = jnp.dot(perm, stage_ref[:, cols], preferred_element_type=F32)
                out_refs[g][0, :, :, cols] = shuffled.reshape(dl, tm // dl, gw).astype(BF16)


def _qkv_layer(x3, mod, layer, n1g, w_qkv, qg, kg, cs, sn):
    batch, s, d = x3.shape
    width = w_qkv.shape[1]
    tm = min(512, s)
    per_b = s // tm
    dils = tuple(dl for _, dl in DIL_PATTERNS)
    gw3 = width // len(dils)
    perms = [_residue_major_perm(tm, dl).astype(BF16) for dl in dils if dl > 1]

    def mspec(k):
        return pl.BlockSpec((1, 1, d), lambda b, i: ((layer * batch + b) * 6 + k, 0, 0))

    small = lambda a: pl.BlockSpec(a.shape, lambda b, i: (0, 0))
    table = pl.BlockSpec((tm, LANES), lambda b, i: (b * per_b + i, 0))
    return pl.pallas_call(
        functools.partial(_qkv_kernel, dils=dils),
        out_shape=tuple(jax.ShapeDtypeStruct((batch, dl, s // dl, gw3), BF16) for dl in dils),
        grid=(batch, per_b),
        in_specs=[
            pl.BlockSpec((1, tm, d), lambda b, i: (b, i, 0)),
            pl.BlockSpec((1, d), lambda b, i: (0, 0)), mspec(0), mspec(1),
            _resident((d, width)), small(qg), small(kg), table, table,
        ] + [_resident((tm, tm)) for _ in perms],
        out_specs=tuple(pl.BlockSpec((1, dl, tm // dl, gw3), lambda b, i: (b, 0, i, 0)) for dl in dils),
        scratch_shapes=[pltpu.VMEM((tm, gw3), BF16)],
        compiler_params=_cparams(2, 60),
        name="qkv_proj",
    )(x3, n1g.reshape(1, d), mod, mod, w_qkv, qg, kg, cs, sn, *perms)


def _attn_kernel(q_ref, kc_ref, vc_ref, kh_ref, vh_ref, o_ref, l_ref, kx_ref, vx_ref, *, qb_rows):
    n = pl.program_id(2)
    nq = qb_rows // SPAN
    kx_ref[0:SPAN, :] = kh_ref[0, 0]
    kx_ref[SPAN:, :] = kc_ref[0, 0]
    vx_ref[0:SPAN, :] = vh_ref[0, 0]
    vx_ref[SPAN:, :] = vc_ref[0, 0]
    row = lax.broadcasted_iota(I32, (SPAN, 2 * SPAN), 0)
    col = lax.broadcasted_iota(I32, (SPAN, 2 * SPAN), 1)
    band = (col >= row) & (col <= row + SPAN)
    lane = lax.broadcasted_iota(I32, (SPAN, LANES), 1)
    scale = HEAD_DIM ** -0.5

    def body(qb, carry):
        r0 = pl.multiple_of(qb * SPAN, SPAN)
        has_prev = (n * nq + qb) > 0
        mask = band & ((col >= SPAN) | has_prev)
        lse_tile = jnp.zeros((SPAN, LANES), F32)
        for hd in range(HEADS):
            cols = slice(hd * HEAD_DIM, (hd + 1) * HEAD_DIM)
            q = q_ref[0, 0, pl.ds(r0, SPAN), cols]
            k = kx_ref[pl.ds(r0, 2 * SPAN), cols]
            v = vx_ref[pl.ds(r0, 2 * SPAN), cols]
            s = lax.dot_general(q, k, (((1,), (1,)), ((), ())), preferred_element_type=F32) * scale
            s = jnp.where(mask, s, NEG_BIG)
            m = jnp.max(s, axis=-1, keepdims=True)
            p = jnp.exp(s - m)
            den = jnp.sum(p, axis=-1, keepdims=True)
            o = jnp.dot(p.astype(BF16), v, preferred_element_type=F32) / den
            o_ref[0, 0, pl.ds(r0, SPAN), cols] = o.astype(BF16)
            lse_tile = jnp.where(lane == hd, m + jnp.log(den), lse_tile)
        l_ref[0, 0, pl.ds(r0, SPAN), :] = lse_tile
        return carry

    lax.fori_loop(0, nq, body, 0)


def _attn_group(qkv_g, group):
    batch, dilation, l, gw3 = qkv_g.shape
    gw = gw3 // 3
    qb_rows = min(512, l)
    sub = qb_rows // SPAN

    def cur(part):
        return pl.BlockSpec((1, 1, qb_rows, gw), lambda b, r, i: (b, r, i, part))

    def halo(part):
        return pl.BlockSpec((1, 1, SPAN, gw), lambda b, r, i: (b, r, jnp.maximum(i * sub - 1, 0), part))

    return pl.pallas_call(
        functools.partial(_attn_kernel, qb_rows=qb_rows),
        out_shape=(jax.ShapeDtypeStruct((batch, dilation, l, gw), BF16),
                   jax.ShapeDtypeStruct((batch, dilation, l, LANES), F32)),
        grid=(batch, dilation, l // qb_rows),
        in_specs=[cur(0), cur(1), cur(2), halo(1), halo(2)],
        out_specs=(pl.BlockSpec((1, 1, qb_rows, gw), lambda b, r, i: (b, r, i, 0)),
                   pl.BlockSpec((1, 1, qb_rows, LANES), lambda b, r, i: (b, r, i, 0))),
        scratch_shapes=[pltpu.VMEM((qb_rows + SPAN, gw), BF16), pltpu.VMEM((qb_rows + SPAN, gw), BF16)],
        compiler_params=_cparams(3, 40),
        name=f"dilated_attn_g{group}",
    )(qkv_g, qkv_g, qkv_g, qkv_g, qkv_g)


def _merge_kernel(*refs, dils):
    ng = len(dils)
    n_perm = sum(1 for dl in dils if dl > 1)
    o_refs, l_refs = refs[:ng], refs[ng:2 * ng]
    x_ref = refs[2 * ng]
    perm_refs = list(refs[2 * ng + 1:2 * ng + 1 + n_perm])
    (wo_ref, g1_ref, n2_ref, sh2_ref, sc2_ref, wr_ref, rb_ref,
     x1_ref, hp_ref, lg_ref) = refs[2 * ng + 1 + n_perm:]
    tm = x_ref.shape[0]
    os_, ls = [], []
    for g, dl in enumerate(dils):
        o = o_refs[g][0].reshape(tm, o_refs[g].shape[-1])
        l = l_refs[g][0].reshape(tm, LANES)
        if dl > 1:
            perm = perm_refs.pop(0)[...]
            o = jnp.dot(perm, o, preferred_element_type=F32)
            l_hi = l.astype(BF16)
            l_mid = (l - l_hi.astype(F32)).astype(BF16)
            l_lo = (l - l_hi.astype(F32) - l_mid.astype(F32)).astype(BF16)
            l = (jnp.dot(perm, l_hi, preferred_element_type=F32)
                 + jnp.dot(perm, l_mid, preferred_element_type=F32)
                 + jnp.dot(perm, l_lo, preferred_element_type=F32))
        os_.append(o)
        ls.append(l)
    m = functools.reduce(jnp.maximum, ls)
    es = [jnp.exp(l - m) for l in ls]
    den = functools.reduce(jnp.add, es)
    alphas = [e / den for e in es]
    parts = []
    for hd in range(HEADS):
        cols = slice(hd * HEAD_DIM, (hd + 1) * HEAD_DIM)
        parts.append(functools.reduce(
            jnp.add, [a[:, hd:hd + 1] * o[:, cols].astype(F32) for a, o in zip(alphas, os_)]))
    o = jnp.concatenate(parts, axis=-1).astype(BF16)
    y = jnp.dot(o, wo_ref[...], preferred_element_type=F32)
    x1 = x_ref[...] + g1_ref[0] * y
    x1_ref[...] = x1
    h2 = _norm_mod(x1, n2_ref[...], sc2_ref[0], sh2_ref[0])
    lg_ref[...] = jnp.dot(h2, wr_ref[...], precision=HIGHEST, preferred_element_type=F32) + rb_ref[...]
    half = h2.shape[1] // 2
    lo = lax.bitcast_convert_type(h2[:, :half].astype(BF16).astype(F32), U32)
    hi = lax.bitcast_convert_type(h2[:, half:].astype(BF16).astype(F32), U32)
    hp_ref[...] = (lo >> 16) | (hi & jnp.uint32(0xFFFF0000))


def _merge_layer(os_, lses, x, mod, layer, batch, w_o, n2g, w_router, b_router):
    n, d = x.shape
    tm = min(512, n)
    per_b = (n // batch) // tm
    ne = w_router.shape[1]
    wr = jnp.zeros((d, LANES), F32).at[:, :ne].set(w_router)
    rb = jnp.zeros((1, LANES), F32).at[0, :ne].set(b_router)

    def mspec(k):
        return pl.BlockSpec((1, 1, d), lambda i: ((layer * batch + i // per_b) * 6 + k, 0, 0))

    tile = lambda w: pl.BlockSpec((tm, w), lambda i: (i, 0))
    vec = lambda w: pl.BlockSpec((1, w), lambda i: (0, 0))
    dils = tuple(o.shape[1] for o in os_)
    planes = lambda dl, w: pl.BlockSpec((1, dl, tm // dl, w), lambda i: (i // per_b, 0, i % per_b, 0))
    unperm = [_residue_major_perm(tm, dl).T.astype(BF16) for dl in dils if dl > 1]
    return pl.pallas_call(
        functools.partial(_merge_kernel, dils=dils),
        out_shape=(jax.ShapeDtypeStruct((n, d), F32), jax.ShapeDtypeStruct((n, d // 2), U32),
                   jax.ShapeDtypeStruct((n, LANES), F32)),
        grid=(n // tm,),
        in_specs=[planes(dl, d) for dl in dils] + [planes(dl, LANES) for dl in dils] + [tile(d)]
                 + [_resident((tm, tm)) for _ in unperm]
                 + [_resident((d, d)), mspec(2), vec(d), mspec(3), mspec(4), _resident((d, LANES)), vec(LANES)],
        out_specs=(tile(d), tile(d // 2), tile(LANES)),
        compiler_params=_cparams(1, 48),
        name="attn_merge_proj",
    )(*os_, *lses, x, *unperm, w_o, mod, n2g.reshape(1, d), mod, mod, wr, rb)


def _route_kernel(lg_ref, route_ref, gate_ref, cnt_ref, carry_ref, *, tr):
    @pl.when(pl.program_id(0) == 0)
    def _():
        carry_ref[...] = jnp.zeros_like(carry_ref)

    lane = lax.broadcasted_iota(I32, (tr, LANES), 1)
    lg = jnp.where(lane < N_EXPERTS, lg_ref[...], NEG_BIG)
    m1 = jnp.max(lg, axis=-1, keepdims=True)
    i1 = jnp.min(jnp.where(lg == m1, lane, LANES), axis=-1, keepdims=True)
    lg2 = jnp.where(lane == i1, NEG_BIG, lg)
    m2 = jnp.max(lg2, axis=-1, keepdims=True)
    i2 = jnp.min(jnp.where(lg2 == m2, lane, LANES), axis=-1, keepdims=True)
    e = jnp.exp(m2 - m1)
    gate_ref[...] = jnp.where(lane == 0, 1.0 / (1.0 + e), jnp.where(lane == 1, e / (1.0 + e), 0.0))

    sel1 = lane == i1
    sel2 = lane == i2
    chosen = jnp.where(sel1 | sel2, 1.0, 0.0)
    r = lax.broadcasted_iota(I32, (tr, tr), 0)
    c = lax.broadcasted_iota(I32, (tr, tr), 1)
    earlier = jnp.where(c < r, 1.0, 0.0).astype(BF16)
    before = jnp.dot(earlier, chosen.astype(BF16), preferred_element_type=F32) + carry_ref[...]
    rank1 = jnp.sum(jnp.where(sel1, before, 0.0), axis=-1, keepdims=True).astype(I32)
    rank2 = jnp.sum(jnp.where(sel2, before, 0.0), axis=-1, keepdims=True).astype(I32)
    route_ref[...] = jnp.where(lane == 0, i1, jnp.where(lane == 1, i2,
                               jnp.where(lane == 2, rank1, jnp.where(lane == 3, rank2, 0))))
    total = carry_ref[...] + jnp.sum(chosen, axis=0, keepdims=True)
    carry_ref[...] = total
    cnt_ref[...] = total.astype(I32)


def _route(logits):
    n = logits.shape[0]
    tr = min(512, n)
    tile = pl.BlockSpec((tr, LANES), lambda i: (i, 0))
    return pl.pallas_call(
        functools.partial(_route_kernel, tr=tr),
        out_shape=(jax.ShapeDtypeStruct((n, LANES), I32), jax.ShapeDtypeStruct((n, LANES), F32),
                   jax.ShapeDtypeStruct((1, LANES), I32)),
        grid=(n // tr,),
        in_specs=[tile],
        out_specs=(tile, tile, pl.BlockSpec((1, LANES), lambda i: (0, 0))),
        scratch_shapes=[pltpu.VMEM((1, LANES), F32)],
        compiler_params=_cparams(1, 32),
        name="moe_route",
    )(logits)


ROW_DMA_UNROLL = 8


def _dispatch_kernel(pos_ref, h_ref, xs_in_ref, xs_ref, sem, *, tm):
    del xs_in_ref

    def row_copy(t, slot):
        return pltpu.make_async_copy(h_ref.at[pl.ds(t, 1)], xs_ref.at[pl.ds(pos_ref[0, slot, t], 1)], sem)

    def start(t, carry):
        row_copy(t, 0).start(priority=0)
        row_copy(t, 1).start(priority=1)
        return carry

    def wait(t, carry):
        row_copy(t, 0).wait()
        row_copy(t, 1).wait()
        return carry

    lax.fori_loop(0, tm, start, 0, unroll=ROW_DMA_UNROLL)
    lax.fori_loop(0, tm, wait, 0, unroll=ROW_DMA_UNROLL)


def _dispatch(h2p, pos_t, rows_padded, tm):
    n, hw = h2p.shape
    xs0 = jnp.zeros((rows_padded, hw), U32)
    return pl.pallas_call(
        functools.partial(_dispatch_kernel, tm=tm),
        out_shape=jax.ShapeDtypeStruct((rows_padded, hw), U32),
        grid=(n // tm,),
        in_specs=[
            pl.BlockSpec((1, 2, tm), lambda i: (i, 0, 0), memory_space=pltpu.SMEM),
            pl.BlockSpec((tm, hw), lambda i: (i, 0)),
            pl.BlockSpec(memory_space=pl.ANY),
        ],
        out_specs=pl.BlockSpec(memory_space=pl.ANY),
        scratch_shapes=[pltpu.SemaphoreType.DMA(())],
        input_output_aliases={2: 0},
        compiler_params=_cparams(1, 32),
        name="moe_dispatch",
    )(pos_t, h2p, xs0)


def _expert_kernel(te_ref, tv_ref, xs_ref, wg_ref, wu_ref, wd_ref, ys_ref, xb_ref, acc_ref, *, nfc, nsub):
    del te_ref
    i = pl.program_id(0)
    j = pl.program_id(1)

    @pl.when((tv_ref[i] == 0) & (j == nfc - 1))
    def _():
        ys_ref[...] = jnp.zeros_like(ys_ref)

    @pl.when(tv_ref[i] == 1)
    def _():
        @pl.when(j == 0)
        def _():
            w = xs_ref[...]
            half = w.shape[1]
            xb_ref[:, :half] = lax.bitcast_convert_type(w << 16, F32).astype(BF16)
            xb_ref[:, half:] = lax.bitcast_convert_type(w & jnp.uint32(0xFFFF0000), F32).astype(BF16)

        xb = xb_ref[...]
        sc = wg_ref.shape[2] // nsub
        part = None
        for s in range(nsub):
            g = jnp.dot(xb, wg_ref[0, :, s * sc:(s + 1) * sc], preferred_element_type=F32)
            u = jnp.dot(xb, wu_ref[0, :, s * sc:(s + 1) * sc], preferred_element_type=F32)
            a = (_silu(g) * u).astype(BF16)
            dn = jnp.dot(a, wd_ref[0, s * sc:(s + 1) * sc, :], preferred_element_type=F32)
            part = dn if part is None else part + dn

        if nfc == 1:
            ys_ref[...] = part
        else:
            @pl.when(j == 0)
            def _():
                acc_ref[...] = part

            if nfc > 2:
                @pl.when((j > 0) & (j < nfc - 1))
                def _():
                    acc_ref[...] += part

            @pl.when(j == nfc - 1)
            def _():
                ys_ref[...] = acc_ref[...] + part


def _expert_ffn(xs, tile_expert, tile_valid, w_gu, w_down, tm):
    rows, hw = xs.shape
    ne, d, two_ff = w_gu.shape
    ff = two_ff // 2
    fc = 1792 if ff % 1792 == 0 else ff
    nfc = ff // fc
    nsub = 2 if fc % (2 * LANES) == 0 else 1
    n_tiles = rows // tm

    def chunk(i, j, tv):
        return jnp.where(tv[i] == 1, j, nfc - 1)

    grid_spec = pltpu.PrefetchScalarGridSpec(
        num_scalar_prefetch=2,
        grid=(n_tiles, nfc),
        in_specs=[
            pl.BlockSpec((tm, hw), lambda i, j, te, tv: (i, 0)),
            pl.BlockSpec((1, d, fc), lambda i, j, te, tv: (te[i], 0, chunk(i, j, tv))),
            pl.BlockSpec((1, d, fc), lambda i, j, te, tv: (te[i], 0, nfc + chunk(i, j, tv))),
            pl.BlockSpec((1, fc, d), lambda i, j, te, tv: (te[i], chunk(i, j, tv), 0)),
        ],
        out_specs=pl.BlockSpec((tm, d), lambda i, j, te, tv: (i, 0)),
        scratch_shapes=[pltpu.VMEM((tm, d), BF16), pltpu.VMEM((tm, d), F32)],
    )
    return pl.pallas_call(
        functools.partial(_expert_kernel, nfc=nfc, nsub=nsub),
        out_shape=jax.ShapeDtypeStruct((rows, d), F32),
        grid_spec=grid_spec,
        compiler_params=_cparams(2, 56),
        name="moe_expert_swiglu",
    )(tile_expert, tile_valid, xs, w_gu, w_gu, w_down)


def _combine_kernel(pos_ref, gate_ref, x_ref, g2_ref, ys_ref, o_ref, ybuf, sem, *, tm):
    def row_copy(t, slot):
        return pltpu.make_async_copy(ys_ref.at[pl.ds(pos_ref[0, slot, t], 1)], ybuf.at[slot, pl.ds(t, 1)], sem)

    def start(t, carry):
        row_copy(t, 0).start(priority=0)
        row_copy(t, 1).start(priority=1)
        return carry

    def wait(t, carry):
        row_copy(t, 0).wait()
        row_copy(t, 1).wait()
        return carry

    lax.fori_loop(0, tm, start, 0, unroll=ROW_DMA_UNROLL)
    lax.fori_loop(0, tm, wait, 0, unroll=ROW_DMA_UNROLL)
    gate = gate_ref[...]
    y = gate[:, 0:1] * ybuf[0] + gate[:, 1:2] * ybuf[1]
    o_ref[...] = x_ref[...] + g2_ref[0] * y


def _combine(ys, pos_t, gates, x1, mod, layer, batch, tm):
    n, d = x1.shape
    per_b = (n // batch) // tm
    return pl.pallas_call(
        functools.partial(_combine_kernel, tm=tm),
        out_shape=jax.ShapeDtypeStruct((n, d), F32),
        grid=(n // tm,),
        in_specs=[
            pl.BlockSpec((1, 2, tm), lambda i: (i, 0, 0), memory_space=pltpu.SMEM),
            pl.BlockSpec((tm, LANES), lambda i: (i, 0)),
            pl.BlockSpec((tm, d), lambda i: (i, 0)),
            pl.BlockSpec((1, 1, d), lambda i: ((layer * batch + i // per_b) * 6 + 5, 0, 0)),
            pl.BlockSpec(memory_space=pl.ANY),
        ],
        out_specs=pl.BlockSpec((tm, d), lambda i: (i, 0)),
        scratch_shapes=[pltpu.VMEM((2, tm, d), F32), pltpu.SemaphoreType.DMA(())],
        compiler_params=_cparams(1, 32),
        name="moe_combine",
    )(pos_t, gates, x1, mod, ys)


def _moe_layer(h2p, logits, x1, mod, layer, batch, w_gu, w_down):
    n = x1.shape[0]
    tm = min(512, n)
    tr = min(512, n)
    route, gates, counts = _route(logits)
    counts = counts[0, :N_EXPERTS]
    padded = ((counts + tm - 1) // tm) * tm
    ends = jnp.cumsum(padded)
    offs = (ends - padded).astype(I32)
    n_tiles = (2 * n) // tm + N_EXPERTS
    rows_padded = n_tiles * tm
    tile_start = jnp.arange(n_tiles, dtype=I32) * tm
    tile_valid = (tile_start < ends[-1]).astype(I32)
    tile_expert = jnp.minimum(jnp.sum(tile_start[:, None] >= ends[None, :], axis=1), N_EXPERTS - 1).astype(I32)
    pos = offs[route[:, 0:2]] + route[:, 2:4]
    pos_t = pos.reshape(n // tr, tr, 2).transpose(0, 2, 1)
    xs = _dispatch(h2p, pos_t, rows_padded, tr)
    ys = _expert_ffn(xs, tile_expert, tile_valid, w_gu, w_down, tm)
    return _combine(ys, pos_t, gates, x1, mod, layer, batch, tr)


def kernel(x, c, positions, norm1_g, norm2_g, ada_w, ada_b, pool_w, pool_scale, attn_w_qkv, attn_w_o,
           q_norm_g, k_norm_g, ffn_w_gu, ffn_w_down, router_w, router_b, moe_w_gu, moe_w_down):
    batch, seq, d = x.shape
    depth = ada_w.shape[0]
    n = batch * seq
    mod = _ada_modulation(c, ada_w, ada_b).reshape(depth * batch * 6, 1, d)
    cs, sn = _rope_tables(positions)
    xf = x.reshape(n, d)
    for layer in range(depth):
        j = layer // 2
        if layer % 2 == 0:
            x1, h2 = _pool_layer(xf.reshape(batch, seq, d), mod, layer, norm1_g[layer], norm2_g[layer],
                                 pool_w[j].astype(BF16), pool_scale[j])
            xf = _ffn_layer(h2.reshape(n, d), x1.reshape(n, d), mod, layer, batch,
                            ffn_w_gu[j].astype(BF16), ffn_w_down[j].astype(BF16))
        else:
            qkvs = _qkv_layer(xf.reshape(batch, seq, d), mod, layer, norm1_g[layer],
                              attn_w_qkv[j].astype(BF16), q_norm_g[j], k_norm_g[j], cs, sn)
            outs = [_attn_group(qkv_g, g) for g, qkv_g in enumerate(qkvs)]
            x1, h2p, logits = _merge_layer([o for o, _ in outs], [l for _, l in outs], xf, mod, layer, batch,
                                           attn_w_o[j].astype(BF16), norm2_g[layer], router_w[j], router_b[j])
            xf = _moe_layer(h2p, logits, x1, mod, layer, batch,
                            moe_w_gu[j].astype(BF16), moe_w_down[j].astype(BF16))
    return xf.reshape(batch, seq, d)
```

```python
import functools

import jax
import jax.numpy as jnp
from jax import lax
from jax.experimental import pallas as pl
from jax.experimental.pallas import tpu as pltpu

F32 = jnp.float32
BF16 = jnp.bfloat16
U32 = jnp.uint32
I32 = jnp.int32

POOL_WINDOWS = (2, 4, 8, 16)
POOL_HALO = 16
DIL_PATTERNS = ((128, 1), (512, 4), (2048, 16))
HEADS = 8
HEAD_DIM = 128
ROPE_DIM = 32
ROPE_THETA = 500000.0
N_EXPERTS = 8
NORM_EPS = 1e-6
NEG_BIG = -1e30

LANES = 128
SPAN = 128
MIB = 1024 * 1024
HIGHEST = lax.Precision.HIGHEST


def _cparams(n_axes, vmem_mib):
    return pltpu.CompilerParams(
        dimension_semantics=("arbitrary",) * n_axes,
        vmem_limit_bytes=int(vmem_mib * MIB),
    )


def _silu(v):
    return v / (1.0 + jnp.exp(-v))


def _norm_mod(t, gain, scale, shift):
    ms = jnp.mean(t * t, axis=-1, keepdims=True)
    return t * lax.rsqrt(ms + NORM_EPS) * gain * (1.0 + scale) + shift


def _resident(shape):
    nd = len(shape)
    return pl.BlockSpec(shape, lambda *_: (0,) * nd, pipeline_mode=pl.Buffered(1))


def _resident_slab(shape, j):
    nd = len(shape)
    return pl.BlockSpec((1,) + tuple(shape), lambda *_: (j,) + (0,) * nd, pipeline_mode=pl.Buffered(1))


def _ada_kernel(c_ref, w_ref, b_ref, o_ref):
    ca = _silu(c_ref[...])
    o_ref[0] = jnp.dot(ca, w_ref[0], precision=HIGHEST, preferred_element_type=F32) + b_ref[0]


def _ada_modulation(c, ada_w, ada_b):
    depth, d, six_d = ada_w.shape
    b = c.shape[0]
    tn = 1536
    return pl.pallas_call(
        _ada_kernel,
        out_shape=jax.ShapeDtypeStruct((depth, b, six_d), F32),
        grid=(depth, six_d // tn),
        in_specs=[
            pl.BlockSpec((b, d), lambda l, j: (0, 0)),
            pl.BlockSpec((1, d, tn), lambda l, j: (l, 0, j)),
            pl.BlockSpec((1, 1, tn), lambda l, j: (l, 0, j)),
        ],
        out_specs=pl.BlockSpec((1, b, tn), lambda l, j: (l, 0, j)),
        compiler_params=_cparams(2, 32),
        name="ada_modulation",
    )(c, ada_w, ada_b.reshape(depth, 1, six_d))


def _rope_kernel(pos_ref, invf_ref, cs_ref, sn_ref):
    ang = pos_ref[...].astype(F32) * invf_ref[...]
    lane = lax.broadcasted_iota(I32, ang.shape, 1)
    half = ROPE_DIM // 2
    c = jnp.cos(ang)
    s = jnp.sin(ang)
    cs_ref[...] = jnp.where(lane < ROPE_DIM, c, 1.0)
    sn_ref[...] = jnp.where(lane < half, -s, jnp.where(lane < ROPE_DIM, s, 0.0))


def _rope_tables(positions):
    n = positions.size
    half = ROPE_DIM // 2
    inv_freq = ROPE_THETA ** (-jnp.arange(half, dtype=F32) * 2.0 / ROPE_DIM)
    invf = jnp.concatenate([inv_freq, inv_freq, jnp.zeros((LANES - ROPE_DIM,), F32)]).reshape(1, LANES)
    tm = min(1024, n)
    return pl.pallas_call(
        _rope_kernel,
        out_shape=(jax.ShapeDtypeStruct((n, LANES), F32),) * 2,
        grid=(n // tm,),
        in_specs=[pl.BlockSpec((tm, 1), lambda i: (i, 0)), pl.BlockSpec((1, LANES), lambda i: (0, 0))],
        out_specs=(pl.BlockSpec((tm, LANES), lambda i: (i, 0)),) * 2,
        compiler_params=_cparams(1, 32),
        name="rope_tables",
    )(positions.reshape(n, 1), invf)


def _pool_kernel(xc_ref, xh_ref, n1_ref, sh1_ref, sc1_ref, g1_ref, pw_ref, ps_ref,
                 n2_ref, sh2_ref, sc2_ref, x1_ref, h2_ref, *, ts):
    i = pl.program_id(1)
    xc = xc_ref[0]
    hc = _norm_mod(xc, n1_ref[...], sc1_ref[0], sh1_ref[0])
    hh = _norm_mod(xh_ref[0], n1_ref[...], sc1_ref[0], sh1_ref[0])
    hh = jnp.where(i > 0, hh, 0.0)
    hcat = jnp.concatenate([hh, hc], axis=0)
    pos = i * ts + lax.broadcasted_iota(I32, (ts, 1), 0)
    gd = hc.shape[1] // len(POOL_WINDOWS)
    ys = []
    for gi, w in enumerate(POOL_WINDOWS):
        s = hcat[:, gi * gd:(gi + 1) * gd]
        span = 1
        while span < w:
            s = s + pltpu.roll(s, span, 0)
            span *= 2
        win = s[POOL_HALO:, :]
        cnt = jnp.minimum(pos + 1, w).astype(F32)
        pooled = win / cnt - hc[:, gi * gd:(gi + 1) * gd]
        ys.append(jnp.dot(pooled.astype(BF16), pw_ref[0, gi], preferred_element_type=F32))
    y = jnp.concatenate(ys, axis=-1) * ps_ref[...]
    x1 = xc + g1_ref[0] * y
    x1_ref[0] = x1
    h2_ref[0] = _norm_mod(x1, n2_ref[...], sc2_ref[0], sh2_ref[0]).astype(BF16)


def _pool_layer(x3, mod, layer, n1g, n2g, pool_w, pool_scale):
    b, s, d = x3.shape
    ts = min(512, s)
    hb = ts // POOL_HALO
    _, ng, gd, _ = pool_w.shape

    def mspec(k):
        return pl.BlockSpec((1, 1, d), lambda bi, i: ((layer * b + bi) * 6 + k, 0, 0))

    vec = pl.BlockSpec((1, d), lambda bi, i: (0, 0))
    tile = pl.BlockSpec((1, ts, d), lambda bi, i: (bi, i, 0))
    return pl.pallas_call(
        functools.partial(_pool_kernel, ts=ts),
        out_shape=(jax.ShapeDtypeStruct((b, s, d), F32), jax.ShapeDtypeStruct((b, s, d), BF16)),
        grid=(b, s // ts),
        in_specs=[
            tile,
            pl.BlockSpec((1, POOL_HALO, d), lambda bi, i: (bi, jnp.maximum(i * hb - 1, 0), 0)),
            vec, mspec(0), mspec(1), mspec(2),
            pl.BlockSpec((1, ng, gd, gd), lambda bi, i: (layer // 2, 0, 0, 0)),
            vec, vec, mspec(3), mspec(4),
        ],
        out_specs=(tile, tile),
        compiler_params=_cparams(2, 48),
        name="pool_mixer",
    )(x3, x3, n1g.reshape(1, d), mod, mod, mod, pool_w, pool_scale.reshape(1, d),
      n2g.reshape(1, d), mod, mod)


def _ffn_kernel(h_ref, x_ref, g2_ref, wgu_ref, wd_ref, o_ref, *, ff, fc):
    h = h_ref[...]
    acc = jnp.zeros(x_ref.shape, F32)
    for c in range(ff // fc):
        g = jnp.dot(h, wgu_ref[0, :, c * fc:(c + 1) * fc], preferred_element_type=F32)
        u = jnp.dot(h, wgu_ref[0, :, ff + c * fc:ff + (c + 1) * fc], preferred_element_type=F32)
        a = (_silu(g) * u).astype(BF16)
        acc = acc + jnp.dot(a, wd_ref[0, c * fc:(c + 1) * fc, :], preferred_element_type=F32)
    o_ref[...] = x_ref[...] + g2_ref[0] * acc


def _ffn_layer(h2, x1, mod, layer, batch, w_gu, w_down):
    n, d = x1.shape
    ff = w_down.shape[1]
    tm = min(512, n)
    per_b = (n // batch) // tm
    fc = ff // 2 if (ff // 2) % LANES == 0 else ff
    tile = lambda: pl.BlockSpec((tm, d), lambda i: (i, 0))
    return pl.pallas_call(
        functools.partial(_ffn_kernel, ff=ff, fc=fc),
        out_shape=jax.ShapeDtypeStruct((n, d), F32),
        grid=(n // tm,),
        in_specs=[
            tile(), tile(),
            pl.BlockSpec((1, 1, d), lambda i: ((layer * batch + i // per_b) * 6 + 5, 0, 0)),
            _resident_slab((d, 2 * ff), layer // 2), _resident_slab((ff, d), layer // 2),
        ],
        out_specs=tile(),
        compiler_params=_cparams(1, 56),
        name="dense_swiglu",
    )(h2, x1, mod, w_gu, w_down)


def _residue_major_perm(tm, dilation):
    per = tm // dilation
    p = jnp.arange(tm)
    src = (p % per) * dilation + p // per
    return (src[:, None] == jnp.arange(tm)[None, :]).astype(F32)


def _qkv_kernel(x_ref, n1_ref, sh1_ref, sc1_ref, w_ref, qg_ref, kg_ref, cs_ref, sn_ref, *rest, dils):
    out_refs = rest[:len(dils)]
    hs_ref = rest[-1]
    h32 = _norm_mod(x_ref[0], n1_ref[...], sc1_ref[0], sh1_ref[0])
    tm, d = h32.shape
    n_slabs = d // LANES
    for c in range(n_slabs):
        hs_ref[c] = h32[:, c * LANES:(c + 1) * LANES]
    first_half = lax.broadcasted_iota(I32, (tm, LANES), 1) < ROPE_DIM // 2
    gw = HEADS * HEAD_DIM
    for g, dl in enumerate(dils):
        per = tm // dl

        def residue_major(ref_2d):
            return jnp.concatenate([ref_2d[pl.ds(r, per, stride=dl), :] for r in range(dl)], axis=0)

        if dl == 1:
            h, cs, sn = h32.astype(BF16), cs_ref[...], sn_ref[...]
        else:
            h = jnp.concatenate([residue_major(hs_ref.at[c]).astype(BF16) for c in range(n_slabs)], axis=1)
            cs, sn = residue_major(cs_ref), residue_major(sn_ref)
        out = out_refs[g]
        for part in range(3):
            blk = g * 3 + part
            y = jnp.dot(h, w_ref[0, :, blk * gw:(blk + 1) * gw], preferred_element_type=F32)
            if part == 2:
                out[0, :, :, part * gw:(part + 1) * gw] = y.astype(BF16).reshape(dl, per, gw)
                continue
            gain = (qg_ref if part == 0 else kg_ref)[g:g + 1, :]
            for hd in range(HEADS):
                yh = y[:, hd * HEAD_DIM:(hd + 1) * HEAD_DIM]
                ms = jnp.mean(yh * yh, axis=-1, keepdims=True)
                yn = yh * lax.rsqrt(ms + NORM_EPS) * gain
                partner = jnp.where(first_half, pltpu.roll(yn, LANES - ROPE_DIM // 2, 1),
                                    pltpu.roll(yn, ROPE_DIM // 2, 1))
                c0 = part * gw + hd * HEAD_DIM
                out[0, :, :, c0:c0 + HEAD_DIM] = (yn * cs + partner * sn).astype(BF16).reshape(dl, per, HEAD_DIM)


def _qkv_layer(x3, mod, layer, n1g, w_qkv, qg, kg, cs, sn):
    batch, s, d = x3.shape
    width = w_qkv.shape[2]
    tm = min(512, s)
    per_b = s // tm
    dils = tuple(dl for _, dl in DIL_PATTERNS)
    gw3 = width // len(dils)

    def mspec(k):
        return pl.BlockSpec((1, 1, d), lambda b, i: ((layer * batch + b) * 6 + k, 0, 0))

    small = lambda a: pl.BlockSpec(a.shape, lambda b, i: (0, 0))
    table = pl.BlockSpec((tm, LANES), lambda b, i: (b * per_b + i, 0))
    return pl.pallas_call(
        functools.partial(_qkv_kernel, dils=dils),
        out_shape=tuple(jax.ShapeDtypeStruct((batch, dl, s // dl, gw3), BF16) for dl in dils),
        grid=(batch, per_b),
        in_specs=[
            pl.BlockSpec((1, tm, d), lambda b, i: (b, i, 0)),
            pl.BlockSpec((1, d), lambda b, i: (0, 0)), mspec(0), mspec(1),
            _resident_slab((d, width), layer // 2), small(qg), small(kg), table, table,
        ],
        out_specs=tuple(pl.BlockSpec((1, dl, tm // dl, gw3), lambda b, i: (b, 0, i, 0)) for dl in dils),
        scratch_shapes=[pltpu.VMEM((d // LANES, tm, LANES), F32)],
        compiler_params=_cparams(2, 60),
        name="qkv_proj",
    )(x3, n1g.reshape(1, d), mod, mod, w_qkv, qg, kg, cs, sn)


def _attn_kernel(q_ref, kc_ref, vc_ref, kh_ref, vh_ref, o_ref, l_ref, kx_ref, vx_ref, *, qb_rows):
    n = pl.program_id(2)
    nq = qb_rows // SPAN
    kx_ref[0:SPAN, :] = kh_ref[0, 0]
    kx_ref[SPAN:, :] = kc_ref[0, 0]
    vx_ref[0:SPAN, :] = vh_ref[0, 0]
    vx_ref[SPAN:, :] = vc_ref[0, 0]
    row = lax.broadcasted_iota(I32, (SPAN, 2 * SPAN), 0)
    col = lax.broadcasted_iota(I32, (SPAN, 2 * SPAN), 1)
    band = (col >= row) & (col <= row + SPAN)
    lane = lax.broadcasted_iota(I32, (SPAN, LANES), 1)
    scale = HEAD_DIM ** -0.5

    def body(qb, carry):
        r0 = pl.multiple_of(qb * SPAN, SPAN)
        has_prev = (n * nq + qb) > 0
        mask = band & ((col >= SPAN) | has_prev)
        lse_tile = jnp.zeros((SPAN, LANES), F32)
        for hd in range(HEADS):
            cols = slice(hd * HEAD_DIM, (hd + 1) * HEAD_DIM)
            q = q_ref[0, 0, pl.ds(r0, SPAN), cols]
            k = kx_ref[pl.ds(r0, 2 * SPAN), cols]
            v = vx_ref[pl.ds(r0, 2 * SPAN), cols]
            s = lax.dot_general(q, k, (((1,), (1,)), ((), ())), preferred_element_type=F32) * scale
            s = jnp.where(mask, s, NEG_BIG)
            m = jnp.max(s, axis=-1, keepdims=True)
            p = jnp.exp(s - m)
            den = jnp.sum(p, axis=-1, keepdims=True)
            o = jnp.dot(p.astype(BF16), v, preferred_element_type=F32) / den
            o_ref[0, 0, pl.ds(r0, SPAN), cols] = o.astype(BF16)
            lse_tile = jnp.where(lane == hd, m + jnp.log(den), lse_tile)
        l_ref[0, 0, pl.ds(r0, SPAN), :] = lse_tile
        return carry

    lax.fori_loop(0, nq, body, 0)


def _attn_group(qkv_g, group):
    batch, dilation, l, gw3 = qkv_g.shape
    gw = gw3 // 3
    qb_rows = min(512, l)
    sub = qb_rows // SPAN

    def cur(part):
        return pl.BlockSpec((1, 1, qb_rows, gw), lambda b, r, i: (b, r, i, part))

    def halo(part):
        return pl.BlockSpec((1, 1, SPAN, gw), lambda b, r, i: (b, r, jnp.maximum(i * sub - 1, 0), part))

    return pl.pallas_call(
        functools.partial(_attn_kernel, qb_rows=qb_rows),
        out_shape=(jax.ShapeDtypeStruct((batch, dilation, l, gw), BF16),
                   jax.ShapeDtypeStruct((batch, dilation, l, LANES), F32)),
        grid=(batch, dilation, l // qb_rows),
        in_specs=[cur(0), cur(1), cur(2), halo(1), halo(2)],
        out_specs=(pl.BlockSpec((1, 1, qb_rows, gw), lambda b, r, i: (b, r, i, 0)),
                   pl.BlockSpec((1, 1, qb_rows, LANES), lambda b, r, i: (b, r, i, 0))),
        scratch_shapes=[pltpu.VMEM((qb_rows + SPAN, gw), BF16), pltpu.VMEM((qb_rows + SPAN, gw), BF16)],
        compiler_params=_cparams(3, 40),
        name=f"dilated_attn_g{group}",
    )(qkv_g, qkv_g, qkv_g, qkv_g, qkv_g)


def _merge_kernel(*refs, dils):
    ng = len(dils)
    n_perm = sum(1 for dl in dils if dl > 1)
    o_refs, l_refs = refs[:ng], refs[ng:2 * ng]
    x_ref = refs[2 * ng]
    perm_refs = list(refs[2 * ng + 1:2 * ng + 1 + n_perm])
    (wo_ref, g1_ref, n2_ref, sh2_ref, sc2_ref, wr_ref, rb_ref,
     x1_ref, hp_ref, lg_ref) = refs[2 * ng + 1 + n_perm:]
    tm = x_ref.shape[0]
    os_, ls = [], []
    for g, dl in enumerate(dils):
        o = o_refs[g][0].reshape(tm, o_refs[g].shape[-1])
        l = l_refs[g][0].reshape(tm, LANES)
        if dl > 1:
            perm = perm_refs.pop(0)[...]
            o = jnp.dot(perm, o, preferred_element_type=F32)
            l_hi = l.astype(BF16)
            l_mid = (l - l_hi.astype(F32)).astype(BF16)
            l_lo = (l - l_hi.astype(F32) - l_mid.astype(F32)).astype(BF16)
            l = (jnp.dot(perm, l_hi, preferred_element_type=F32)
                 + jnp.dot(perm, l_mid, preferred_element_type=F32)
                 + jnp.dot(perm, l_lo, preferred_element_type=F32))
        os_.append(o)
        ls.append(l)
    m = functools.reduce(jnp.maximum, ls)
    es = [jnp.exp(l - m) for l in ls]
    den = functools.reduce(jnp.add, es)
    alphas = [e / den for e in es]
    parts = []
    for hd in range(HEADS):
        cols = slice(hd * HEAD_DIM, (hd + 1) * HEAD_DIM)
        parts.append(functools.reduce(
            jnp.add, [a[:, hd:hd + 1] * o[:, cols].astype(F32) for a, o in zip(alphas, os_)]))
    o = jnp.concatenate(parts, axis=-1).astype(BF16)
    y = jnp.dot(o, wo_ref[0], preferred_element_type=F32)
    x1 = x_ref[...] + g1_ref[0] * y
    x1_ref[...] = x1
    h2 = _norm_mod(x1, n2_ref[...], sc2_ref[0], sh2_ref[0])
    h_hi = h2.astype(BF16)
    h_hi32 = h_hi.astype(F32)
    h_lo = (h2 - h_hi32).astype(BF16)
    lg_ref[...] = (jnp.dot(h_hi, wr_ref[0], preferred_element_type=F32)
                   + jnp.dot(h_lo, wr_ref[0], preferred_element_type=F32)
                   + jnp.dot(h_hi, wr_ref[1], preferred_element_type=F32) + rb_ref[...])
    half = h2.shape[1] // 2
    lo = lax.bitcast_convert_type(h_hi32[:, :half], U32)
    hi = lax.bitcast_convert_type(h_hi32[:, half:], U32)
    hp_ref[...] = (lo >> 16) | (hi & jnp.uint32(0xFFFF0000))


def _merge_layer(os_, lses, x, mod, layer, batch, w_o, n2g, w_router, b_router):
    n, d = x.shape
    tm = min(512, n)
    per_b = (n // batch) // tm
    ne = w_router.shape[1]
    wr = jnp.zeros((d, LANES), F32).at[:, :ne].set(w_router)
    wr_hi = wr.astype(BF16)
    wr = jnp.stack([wr_hi, (wr - wr_hi.astype(F32)).astype(BF16)])
    rb = jnp.zeros((1, LANES), F32).at[0, :ne].set(b_router)

    def mspec(k):
        return pl.BlockSpec((1, 1, d), lambda i: ((layer * batch + i // per_b) * 6 + k, 0, 0))

    tile = lambda w: pl.BlockSpec((tm, w), lambda i: (i, 0))
    vec = lambda w: pl.BlockSpec((1, w), lambda i: (0, 0))
    dils = tuple(o.shape[1] for o in os_)
    planes = lambda dl, w: pl.BlockSpec((1, dl, tm // dl, w), lambda i: (i // per_b, 0, i % per_b, 0))
    unperm = [_residue_major_perm(tm, dl).T.astype(BF16) for dl in dils if dl > 1]
    return pl.pallas_call(
        functools.partial(_merge_kernel, dils=dils),
        out_shape=(jax.ShapeDtypeStruct((n, d), F32), jax.ShapeDtypeStruct((n, d // 2), U32),
                   jax.ShapeDtypeStruct((n, LANES), F32)),
        grid=(n // tm,),
        in_specs=[planes(dl, d) for dl in dils] + [planes(dl, LANES) for dl in dils] + [tile(d)]
                 + [_resident((tm, tm)) for _ in unperm]
                 + [_resident_slab((d, d), layer // 2), mspec(2), vec(d), mspec(3), mspec(4), _resident((2, d, LANES)), vec(LANES)],
        out_specs=(tile(d), tile(d // 2), tile(LANES)),
        compiler_params=_cparams(1, 48),
        name="attn_merge_proj",
    )(*os_, *lses, x, *unperm, w_o, mod, n2g.reshape(1, d), mod, mod, wr, rb)


def _route_kernel(lg_ref, route_ref, gate_ref, cnt_ref, carry_ref, *, tr):
    @pl.when(pl.program_id(0) == 0)
    def _():
        carry_ref[...] = jnp.zeros_like(carry_ref)

    lane = lax.broadcasted_iota(I32, (tr, LANES), 1)
    lg = jnp.where(lane < N_EXPERTS, lg_ref[...], NEG_BIG)
    m1 = jnp.max(lg, axis=-1, keepdims=True)
    i1 = jnp.min(jnp.where(lg == m1, lane, LANES), axis=-1, keepdims=True)
    lg2 = jnp.where(lane == i1, NEG_BIG, lg)
    m2 = jnp.max(lg2, axis=-1, keepdims=True)
    i2 = jnp.min(jnp.where(lg2 == m2, lane, LANES), axis=-1, keepdims=True)
    e = jnp.exp(m2 - m1)
    gate_ref[...] = jnp.where(lane == 0, 1.0 / (1.0 + e), jnp.where(lane == 1, e / (1.0 + e), 0.0))

    sel1 = lane == i1
    sel2 = lane == i2
    chosen = jnp.where(sel1 | sel2, 1.0, 0.0)
    r = lax.broadcasted_iota(I32, (tr, tr), 0)
    c = lax.broadcasted_iota(I32, (tr, tr), 1)
    earlier = jnp.where(c < r, 1.0, 0.0).astype(BF16)
    before = jnp.dot(earlier, chosen.astype(BF16), preferred_element_type=F32) + carry_ref[...]
    rank1 = jnp.sum(jnp.where(sel1, before, 0.0), axis=-1, keepdims=True).astype(I32)
    rank2 = jnp.sum(jnp.where(sel2, before, 0.0), axis=-1, keepdims=True).astype(I32)
    route_ref[...] = jnp.where(lane == 0, i1, jnp.where(lane == 1, i2,
                               jnp.where(lane == 2, rank1, jnp.where(lane == 3, rank2, 0))))
    total = carry_ref[...] + jnp.sum(chosen, axis=0, keepdims=True)
    carry_ref[...] = total
    cnt_ref[...] = total.astype(I32)


def _route(logits):
    n = logits.shape[0]
    tr = min(512, n)
    tile = pl.BlockSpec((tr, LANES), lambda i: (i, 0))
    return pl.pallas_call(
        functools.partial(_route_kernel, tr=tr),
        out_shape=(jax.ShapeDtypeStruct((n, LANES), I32), jax.ShapeDtypeStruct((n, LANES), F32),
                   jax.ShapeDtypeStruct((1, LANES), I32)),
        grid=(n // tr,),
        in_specs=[tile],
        out_specs=(tile, tile, pl.BlockSpec((1, LANES), lambda i: (0, 0))),
        scratch_shapes=[pltpu.VMEM((1, LANES), F32)],
        compiler_params=_cparams(1, 32),
        name="moe_route",
    )(logits)


ROW_DMA_GROUP = 8


def _row_dma_loop(tm, pos_ref, row_copy):
    def start(c, carry):
        base = pl.multiple_of(c * ROW_DMA_GROUP, ROW_DMA_GROUP)
        where = [[pos_ref[0, slot, base + k] for slot in range(2)] for k in range(ROW_DMA_GROUP)]
        for k in range(ROW_DMA_GROUP):
            for slot in range(2):
                row_copy(base + k, slot, where[k][slot]).start(priority=slot)
        return carry

    def wait(c, carry):
        base = pl.multiple_of(c * ROW_DMA_GROUP, ROW_DMA_GROUP)
        for k in range(ROW_DMA_GROUP):
            for slot in range(2):
                row_copy(base + k, slot, 0).wait()
        return carry

    lax.fori_loop(0, tm // ROW_DMA_GROUP, start, 0)
    lax.fori_loop(0, tm // ROW_DMA_GROUP, wait, 0)


def _dispatch_kernel(pos_ref, h_ref, xs_in_ref, xs_ref, sem, *, tm):
    del xs_in_ref

    def row_copy(t, slot, pos):
        del slot
        return pltpu.make_async_copy(h_ref.at[pl.ds(t, 1)], xs_ref.at[pl.ds(pos, 1)], sem)

    _row_dma_loop(tm, pos_ref, row_copy)


def _dispatch(h2p, pos_t, rows_padded, tm):
    n, hw = h2p.shape
    xs0 = jnp.zeros((rows_padded, hw), U32)
    return pl.pallas_call(
        functools.partial(_dispatch_kernel, tm=tm),
        out_shape=jax.ShapeDtypeStruct((rows_padded, hw), U32),
        grid=(n // tm,),
        in_specs=[
            pl.BlockSpec((1, 2, tm), lambda i: (i, 0, 0), memory_space=pltpu.SMEM),
            pl.BlockSpec((tm, hw), lambda i: (i, 0)),
            pl.BlockSpec(memory_space=pl.ANY),
        ],
        out_specs=pl.BlockSpec(memory_space=pl.ANY),
        scratch_shapes=[pltpu.SemaphoreType.DMA(())],
        input_output_aliases={2: 0},
        compiler_params=_cparams(1, 32),
        name="moe_dispatch",
    )(pos_t, h2p, xs0)


def _expert_kernel(te_ref, tv_ref, xs_ref, wg_ref, wu_ref, wd_ref, ys_ref, xb_ref, acc_ref, *, nfc, nsub):
    del te_ref
    i = pl.program_id(0)
    j = pl.program_id(1)

    @pl.when((tv_ref[i] == 0) & (j == nfc - 1))
    def _():
        ys_ref[...] = jnp.zeros_like(ys_ref)

    @pl.when(tv_ref[i] == 1)
    def _():
        @pl.when(j == 0)
        def _():
            w = xs_ref[...]
            half = w.shape[1]
            xb_ref[:, :half] = lax.bitcast_convert_type(w << 16, F32).astype(BF16)
            xb_ref[:, half:] = lax.bitcast_convert_type(w & jnp.uint32(0xFFFF0000), F32).astype(BF16)

        xb = xb_ref[...]
        sc = wg_ref.shape[2] // nsub
        part = None
        for s in range(nsub):
            g = jnp.dot(xb, wg_ref[0, :, s * sc:(s + 1) * sc], preferred_element_type=F32)
            u = jnp.dot(xb, wu_ref[0, :, s * sc:(s + 1) * sc], preferred_element_type=F32)
            a = (_silu(g) * u).astype(BF16)
            dn = jnp.dot(a, wd_ref[0, s * sc:(s + 1) * sc, :], preferred_element_type=F32)
            part = dn if part is None else part + dn

        if nfc == 1:
            ys_ref[...] = part
        else:
            @pl.when(j == 0)
            def _():
                acc_ref[...] = part

            if nfc > 2:
                @pl.when((j > 0) & (j < nfc - 1))
                def _():
                    acc_ref[...] += part

            @pl.when(j == nfc - 1)
            def _():
                ys_ref[...] = acc_ref[...] + part


def _expert_ffn(xs, tile_expert, tile_valid, w_gu, w_down, tm):
    rows, hw = xs.shape
    ne, d, two_ff = w_gu.shape
    ff = two_ff // 2
    fc = 1792 if ff % 1792 == 0 else ff
    nfc = ff // fc
    nsub = 2 if fc % (2 * LANES) == 0 else 1
    n_tiles = rows // tm

    def chunk(i, j, tv):
        return jnp.where(tv[i] == 1, j, nfc - 1)

    grid_spec = pltpu.PrefetchScalarGridSpec(
        num_scalar_prefetch=2,
        grid=(n_tiles, nfc),
        in_specs=[
            pl.BlockSpec((tm, hw), lambda i, j, te, tv: (i, 0)),
            pl.BlockSpec((1, d, fc), lambda i, j, te, tv: (te[i], 0, chunk(i, j, tv))),
            pl.BlockSpec((1, d, fc), lambda i, j, te, tv: (te[i], 0, nfc + chunk(i, j, tv))),
            pl.BlockSpec((1, fc, d), lambda i, j, te, tv: (te[i], chunk(i, j, tv), 0)),
        ],
        out_specs=pl.BlockSpec((tm, d), lambda i, j, te, tv: (i, 0)),
        scratch_shapes=[pltpu.VMEM((tm, d), BF16), pltpu.VMEM((tm, d), F32)],
    )
    return pl.pallas_call(
        functools.partial(_expert_kernel, nfc=nfc, nsub=nsub),
        out_shape=jax.ShapeDtypeStruct((rows, d), F32),
        grid_spec=grid_spec,
        compiler_params=_cparams(2, 56),
        name="moe_expert_swiglu",
    )(tile_expert, tile_valid, xs, w_gu, w_gu, w_down)


def _combine_kernel(pos_ref, gate_ref, x_ref, g2_ref, ys_ref, o_ref, ybuf, sem, *, tm):
    def row_copy(t, slot, pos):
        return pltpu.make_async_copy(ys_ref.at[pl.ds(pos, 1)], ybuf.at[slot, pl.ds(t, 1)], sem)

    _row_dma_loop(tm, pos_ref, row_copy)
    gate = gate_ref[...]
    y = gate[:, 0:1] * ybuf[0] + gate[:, 1:2] * ybuf[1]
    o_ref[...] = x_ref[...] + g2_ref[0] * y


def _combine(ys, pos_t, gates, x1, mod, layer, batch, tm):
    n, d = x1.shape
    per_b = (n // batch) // tm
    return pl.pallas_call(
        functools.partial(_combine_kernel, tm=tm),
        out_shape=jax.ShapeDtypeStruct((n, d), F32),
        grid=(n // tm,),
        in_specs=[
            pl.BlockSpec((1, 2, tm), lambda i: (i, 0, 0), memory_space=pltpu.SMEM),
            pl.BlockSpec((tm, LANES), lambda i: (i, 0)),
            pl.BlockSpec((tm, d), lambda i: (i, 0)),
            pl.BlockSpec((1, 1, d), lambda i: ((layer * batch + i // per_b) * 6 + 5, 0, 0)),
            pl.BlockSpec(memory_space=pl.ANY),
        ],
        out_specs=pl.BlockSpec((tm, d), lambda i: (i, 0)),
        scratch_shapes=[pltpu.VMEM((2, tm, d), F32), pltpu.SemaphoreType.DMA(())],
        compiler_params=_cparams(1, 32),
        name="moe_combine",
    )(pos_t, gates, x1, mod, ys)


def _moe_layer(h2p, logits, x1, mod, layer, batch, w_gu, w_down):
    n = x1.shape[0]
    tm = min(512, n)
    tr = min(512, n)
    route, gates, counts = _route(logits)
    counts = counts[0, :N_EXPERTS]
    padded = ((counts + tm - 1) // tm) * tm
    ends = jnp.cumsum(padded)
    offs = (ends - padded).astype(I32)
    n_tiles = (2 * n) // tm + N_EXPERTS
    rows_padded = n_tiles * tm
    tile_start = jnp.arange(n_tiles, dtype=I32) * tm
    tile_valid = (tile_start < ends[-1]).astype(I32)
    tile_expert = jnp.minimum(jnp.sum(tile_start[:, None] >= ends[None, :], axis=1), N_EXPERTS - 1).astype(I32)
    tile_expert = tile_expert + (layer // 2) * N_EXPERTS
    pos = offs[route[:, 0:2]] + route[:, 2:4]
    pos_t = pos.reshape(n // tr, tr, 2).transpose(0, 2, 1)
    xs = _dispatch(h2p, pos_t, rows_padded, tr)
    ys = _expert_ffn(xs, tile_expert, tile_valid, w_gu, w_down, tm)
    return _combine(ys, pos_t, gates, x1, mod, layer, batch, tr)


def kernel(x, c, positions, norm1_g, norm2_g, ada_w, ada_b, pool_w, pool_scale, attn_w_qkv, attn_w_o,
           q_norm_g, k_norm_g, ffn_w_gu, ffn_w_down, router_w, router_b, moe_w_gu, moe_w_down):
    batch, seq, d = x.shape
    depth = ada_w.shape[0]
    n = batch * seq
    mod = _ada_modulation(c, ada_w, ada_b).reshape(depth * batch * 6, 1, d)
    cs, sn = _rope_tables(positions)
    pool_wb, ffn_gub, ffn_db = pool_w.astype(BF16), ffn_w_gu.astype(BF16), ffn_w_down.astype(BF16)
    qkv_wb, wo_b = attn_w_qkv.astype(BF16), attn_w_o.astype(BF16)
    moe_gub = moe_w_gu.astype(BF16).reshape((-1,) + moe_w_gu.shape[2:])
    moe_db = moe_w_down.astype(BF16).reshape((-1,) + moe_w_down.shape[2:])
    xf = x.reshape(n, d)
    for layer in range(depth):
        j = layer // 2
        if layer % 2 == 0:
            x1, h2 = _pool_layer(xf.reshape(batch, seq, d), mod, layer, norm1_g[layer], norm2_g[layer],
                                 pool_wb, pool_scale[j])
            xf = _ffn_layer(h2.reshape(n, d), x1.reshape(n, d), mod, layer, batch, ffn_gub, ffn_db)
        else:
            qkvs = _qkv_layer(xf.reshape(batch, seq, d), mod, layer, norm1_g[layer],
                              qkv_wb, q_norm_g[j], k_norm_g[j], cs, sn)
            outs = [_attn_group(qkv_g, g) for g, qkv_g in enumerate(qkvs)]
            x1, h2p, logits = _merge_layer([o for o, _ in outs], [l for _, l in outs], xf, mod, layer, batch,
                                           wo_b, norm2_g[layer], router_w[j], router_b[j])
            xf = _moe_layer(h2p, logits, x1, mod, layer, batch, moe_gub, moe_db)
    return xf.reshape(batch, seq, d)
```

```python
import functools

import jax
import jax.numpy as jnp
from jax import lax
from jax.experimental import pallas as pl
from jax.experimental.pallas import tpu as pltpu

F32 = jnp.float32
BF16 = jnp.bfloat16
U32 = jnp.uint32
I32 = jnp.int32

POOL_WINDOWS = (2, 4, 8, 16)
POOL_HALO = 16
DIL_PATTERNS = ((128, 1), (512, 4), (2048, 16))
HEADS = 8
HEAD_DIM = 128
ROPE_DIM = 32
ROPE_THETA = 500000.0
N_EXPERTS = 8
NORM_EPS = 1e-6
NEG_BIG = -1e30
LOG2_E = 1.4426950408889634

LANES = 128
SPAN = 128
MIB = 1024 * 1024
HIGHEST = lax.Precision.HIGHEST


def _cparams(n_axes, vmem_mib):
    return pltpu.CompilerParams(
        dimension_semantics=("arbitrary",) * n_axes,
        vmem_limit_bytes=int(vmem_mib * MIB),
    )


def _silu(v):
    return v / (1.0 + jnp.exp(-v))


def _norm_mod(t, gain, scale, shift):
    ms = jnp.mean(t * t, axis=-1, keepdims=True)
    return t * lax.rsqrt(ms + NORM_EPS) * gain * (1.0 + scale) + shift


def _resident(shape):
    nd = len(shape)
    return pl.BlockSpec(shape, lambda *_: (0,) * nd, pipeline_mode=pl.Buffered(1))


def _resident_slab(shape, j):
    nd = len(shape)
    return pl.BlockSpec((1,) + tuple(shape), lambda *_: (j,) + (0,) * nd, pipeline_mode=pl.Buffered(1))


def _ada_kernel(c_ref, w_ref, b_ref, o_ref):
    ca = _silu(c_ref[...])
    o_ref[0] = jnp.dot(ca, w_ref[0], precision=HIGHEST, preferred_element_type=F32) + b_ref[0]


def _ada_modulation(c, ada_w, ada_b):
    depth, d, six_d = ada_w.shape
    b = c.shape[0]
    tn = 1536
    return pl.pallas_call(
        _ada_kernel,
        out_shape=jax.ShapeDtypeStruct((depth, b, six_d), F32),
        grid=(depth, six_d // tn),
        in_specs=[
            pl.BlockSpec((b, d), lambda l, j: (0, 0)),
            pl.BlockSpec((1, d, tn), lambda l, j: (l, 0, j)),
            pl.BlockSpec((1, 1, tn), lambda l, j: (l, 0, j)),
        ],
        out_specs=pl.BlockSpec((1, b, tn), lambda l, j: (l, 0, j)),
        compiler_params=_cparams(2, 32),
        name="ada_modulation",
    )(c, ada_w, ada_b.reshape(depth, 1, six_d))


ROPE_HALF = ROPE_DIM // 2
PARTNER_SHIFT = LANES // 2


def _rotary_head_layout(t):
    return jnp.concatenate([t[..., :ROPE_HALF], t[..., ROPE_DIM:ROPE_DIM + PARTNER_SHIFT - ROPE_HALF],
                            t[..., ROPE_HALF:ROPE_DIM], t[..., ROPE_DIM + PARTNER_SHIFT - ROPE_HALF:]], axis=-1)


def _rope_kernel(pos_ref, invf_ref, cs_ref, sn_ref):
    ang = pos_ref[...].astype(F32) * invf_ref[...]
    lane = lax.broadcasted_iota(I32, ang.shape, 1)
    lo = lane < ROPE_HALF
    hi = (lane >= PARTNER_SHIFT) & (lane < PARTNER_SHIFT + ROPE_HALF)
    c = jnp.cos(ang)
    s = jnp.sin(ang)
    cs_ref[...] = jnp.where(lo | hi, c, 1.0)
    sn_ref[...] = jnp.where(lo, -s, jnp.where(hi, s, 0.0))


def _rope_tables(positions):
    n = positions.size
    inv_freq = ROPE_THETA ** (-jnp.arange(ROPE_HALF, dtype=F32) * 2.0 / ROPE_DIM)
    gap = jnp.zeros((PARTNER_SHIFT - ROPE_HALF,), F32)
    invf = jnp.concatenate([inv_freq, gap, inv_freq, gap]).reshape(1, LANES)
    tm = min(1024, n)
    return pl.pallas_call(
        _rope_kernel,
        out_shape=(jax.ShapeDtypeStruct((n, LANES), F32),) * 2,
        grid=(n // tm,),
        in_specs=[pl.BlockSpec((tm, 1), lambda i: (i, 0)), pl.BlockSpec((1, LANES), lambda i: (0, 0))],
        out_specs=(pl.BlockSpec((tm, LANES), lambda i: (i, 0)),) * 2,
        compiler_params=_cparams(1, 32),
        name="rope_tables",
    )(positions.reshape(n, 1), invf)


def _pool_kernel(xc_ref, xh_ref, n1_ref, sh1_ref, sc1_ref, g1_ref, pw_ref, ps_ref,
                 n2_ref, sh2_ref, sc2_ref, x1_ref, h2_ref, *, ts):
    i = pl.program_id(1)
    xc = xc_ref[0]
    hc = _norm_mod(xc, n1_ref[...], sc1_ref[0], sh1_ref[0])
    hh = _norm_mod(xh_ref[0], n1_ref[...], sc1_ref[0], sh1_ref[0])
    hh = jnp.where(i > 0, hh, 0.0)
    hcat = jnp.concatenate([hh, hc], axis=0)
    pos = i * ts + lax.broadcasted_iota(I32, (ts, 1), 0)
    gd = hc.shape[1] // len(POOL_WINDOWS)
    ys = []
    for gi, w in enumerate(POOL_WINDOWS):
        s = hcat[:, gi * gd:(gi + 1) * gd]
        span = 1
        while span < w:
            s = s + pltpu.roll(s, span, 0)
            span *= 2
        win = s[POOL_HALO:, :]
        cnt = jnp.minimum(pos + 1, w).astype(F32)
        pooled = win / cnt - hc[:, gi * gd:(gi + 1) * gd]
        ys.append(jnp.dot(pooled.astype(BF16), pw_ref[0, gi], preferred_element_type=F32))
    y = jnp.concatenate(ys, axis=-1) * ps_ref[...]
    x1 = xc + g1_ref[0] * y
    x1_ref[0] = x1
    h2_ref[0] = _norm_mod(x1, n2_ref[...], sc2_ref[0], sh2_ref[0]).astype(BF16)


def _pool_layer(x3, mod, layer, n1g, n2g, pool_w, pool_scale):
    b, s, d = x3.shape
    ts = min(512, s)
    hb = ts // POOL_HALO
    _, ng, gd, _ = pool_w.shape

    def mspec(k):
        return pl.BlockSpec((1, 1, d), lambda bi, i: ((layer * b + bi) * 6 + k, 0, 0))

    vec = pl.BlockSpec((1, d), lambda bi, i: (0, 0))
    tile = pl.BlockSpec((1, ts, d), lambda bi, i: (bi, i, 0))
    return pl.pallas_call(
        functools.partial(_pool_kernel, ts=ts),
        out_shape=(jax.ShapeDtypeStruct((b, s, d), F32), jax.ShapeDtypeStruct((b, s, d), BF16)),
        grid=(b, s // ts),
        in_specs=[
            tile,
            pl.BlockSpec((1, POOL_HALO, d), lambda bi, i: (bi, jnp.maximum(i * hb - 1, 0), 0)),
            vec, mspec(0), mspec(1), mspec(2),
            pl.BlockSpec((1, ng, gd, gd), lambda bi, i: (layer // 2, 0, 0, 0)),
            vec, vec, mspec(3), mspec(4),
        ],
        out_specs=(tile, tile),
        compiler_params=_cparams(2, 48),
        name="pool_mixer",
    )(x3, x3, n1g.reshape(1, d), mod, mod, mod, pool_w, pool_scale.reshape(1, d),
      n2g.reshape(1, d), mod, mod)


def _ffn_kernel(h_ref, x_ref, g2_ref, wgu_ref, wd_ref, o_ref, *, ff, fc):
    h = h_ref[...]
    acc = jnp.zeros(x_ref.shape, F32)
    for c in range(ff // fc):
        g = jnp.dot(h, wgu_ref[0, :, c * fc:(c + 1) * fc], preferred_element_type=F32)
        u = jnp.dot(h, wgu_ref[0, :, ff + c * fc:ff + (c + 1) * fc], preferred_element_type=F32)
        a = (_silu(g) * u).astype(BF16)
        acc = acc + jnp.dot(a, wd_ref[0, c * fc:(c + 1) * fc, :], preferred_element_type=F32)
    o_ref[...] = x_ref[...] + g2_ref[0] * acc


def _ffn_layer(h2, x1, mod, layer, batch, w_gu, w_down):
    n, d = x1.shape
    ff = w_down.shape[1]
    tm = min(512, n)
    per_b = (n // batch) // tm
    fc = ff // 2 if (ff // 2) % LANES == 0 else ff
    tile = lambda: pl.BlockSpec((tm, d), lambda i: (i, 0))
    return pl.pallas_call(
        functools.partial(_ffn_kernel, ff=ff, fc=fc),
        out_shape=jax.ShapeDtypeStruct((n, d), F32),
        grid=(n // tm,),
        in_specs=[
            tile(), tile(),
            pl.BlockSpec((1, 1, d), lambda i: ((layer * batch + i // per_b) * 6 + 5, 0, 0)),
            _resident_slab((d, 2 * ff), layer // 2), _resident_slab((ff, d), layer // 2),
        ],
        out_specs=tile(),
        compiler_params=_cparams(1, 56),
        name="dense_swiglu",
    )(h2, x1, mod, w_gu, w_down)


def _residue_major_perm(tm, dilation):
    per = tm // dilation
    p = jnp.arange(tm)
    src = (p % per) * dilation + p // per
    return (src[:, None] == jnp.arange(tm)[None, :]).astype(F32)


def _qkv_kernel(x_ref, n1_ref, sh1_ref, sc1_ref, w_ref, qg_ref, kg_ref, cs_ref, sn_ref, *rest, dils):
    out_refs = rest[:len(dils)]
    hs_ref = rest[-1]
    h32 = _norm_mod(x_ref[0], n1_ref[...], sc1_ref[0], sh1_ref[0])
    tm, d = h32.shape
    n_slabs = d // LANES
    for c in range(n_slabs):
        hs_ref[c] = h32[:, c * LANES:(c + 1) * LANES]
    gw = HEADS * HEAD_DIM

    def residue_major(ref_2d, dl):
        return jnp.concatenate([ref_2d[pl.ds(r, tm // dl, stride=dl), :] for r in range(dl)], axis=0)

    hs = []
    for g, dl in enumerate(dils):
        per = tm // dl
        if dl == 1:
            h, cs, sn = h32.astype(BF16), cs_ref[...], sn_ref[...]
        else:
            h = jnp.concatenate([residue_major(hs_ref.at[c], dl).astype(BF16) for c in range(n_slabs)], axis=1)
            cs, sn = residue_major(cs_ref, dl), residue_major(sn_ref, dl)
        hs.append(h)
        for part in range(2):
            blk = g * 3 + part
            y = jnp.dot(h, w_ref[0, :, blk * gw:(blk + 1) * gw], preferred_element_type=F32)
            gain = (qg_ref if part == 0 else kg_ref)[g:g + 1, :]
            for hd in range(HEADS):
                yh = y[:, hd * HEAD_DIM:(hd + 1) * HEAD_DIM]
                ms = jnp.mean(yh * yh, axis=-1, keepdims=True)
                yn = yh * lax.rsqrt(ms + NORM_EPS) * gain
                partner = pltpu.roll(yn, PARTNER_SHIFT, 1)
                c0 = part * gw + hd * HEAD_DIM
                out_refs[g][0, :, :, c0:c0 + HEAD_DIM] = (
                    (yn * cs + partner * sn).astype(BF16).reshape(dl, per, HEAD_DIM))
    for g, dl in enumerate(dils):
        blk = g * 3 + 2
        y = jnp.dot(hs[g], w_ref[0, :, blk * gw:(blk + 1) * gw], preferred_element_type=F32)
        out_refs[g][0, :, :, 2 * gw:3 * gw] = y.astype(BF16).reshape(dl, tm // dl, gw)


def _qkv_layer(x3, mod, layer, n1g, w_qkv, qg, kg, cs, sn):
    batch, s, d = x3.shape
    width = w_qkv.shape[2]
    tm = min(512, s)
    per_b = s // tm
    dils = tuple(dl for _, dl in DIL_PATTERNS)
    gw3 = width // len(dils)

    def mspec(k):
        return pl.BlockSpec((1, 1, d), lambda b, i: ((layer * batch + b) * 6 + k, 0, 0))

    small = lambda a: pl.BlockSpec(a.shape, lambda b, i: (0, 0))
    table = pl.BlockSpec((tm, LANES), lambda b, i: (b * per_b + i, 0))
    return pl.pallas_call(
        functools.partial(_qkv_kernel, dils=dils),
        out_shape=tuple(jax.ShapeDtypeStruct((batch, dl, s // dl, gw3), BF16) for dl in dils),
        grid=(batch, per_b),
        in_specs=[
            pl.BlockSpec((1, tm, d), lambda b, i: (b, i, 0)),
            pl.BlockSpec((1, d), lambda b, i: (0, 0)), mspec(0), mspec(1),
            _resident_slab((d, width), layer // 2), small(qg), small(kg), table, table,
        ],
        out_specs=tuple(pl.BlockSpec((1, dl, tm // dl, gw3), lambda b, i: (b, 0, i, 0)) for dl in dils),
        scratch_shapes=[pltpu.VMEM((d // LANES, tm, LANES), F32)],
        compiler_params=_cparams(2, 60),
        name="qkv_proj",
    )(x3, n1g.reshape(1, d), mod, mod, w_qkv, qg, kg, cs, sn)


def _attn_kernel(q_ref, kc_ref, vc_ref, kh_ref, vh_ref, o_ref, l_ref, kx_ref, vx_ref, *, qb_rows):
    n = pl.program_id(2)
    nq = qb_rows // SPAN
    kx_ref[0:SPAN, :] = kh_ref[0, 0]
    kx_ref[SPAN:, :] = kc_ref[0, 0]
    vx_ref[0:SPAN, :] = vh_ref[0, 0]
    vx_ref[SPAN:, :] = vc_ref[0, 0]
    row = lax.broadcasted_iota(I32, (SPAN, 2 * SPAN), 0)
    col = lax.broadcasted_iota(I32, (SPAN, 2 * SPAN), 1)
    band = (col >= row) & (col <= row + SPAN)
    lane = lax.broadcasted_iota(I32, (SPAN, LANES), 1)
    scale = HEAD_DIM ** -0.5

    def body(qb, carry):
        r0 = pl.multiple_of(qb * SPAN, SPAN)
        has_prev = (n * nq + qb) > 0
        mask = band & ((col >= SPAN) | has_prev)
        lse_tile = jnp.zeros((SPAN, LANES), F32)
        for hd in range(HEADS):
            cols = slice(hd * HEAD_DIM, (hd + 1) * HEAD_DIM)
            q = q_ref[0, 0, pl.ds(r0, SPAN), cols]
            k = kx_ref[pl.ds(r0, 2 * SPAN), cols]
            v = vx_ref[pl.ds(r0, 2 * SPAN), cols]
            s = lax.dot_general(q, k, (((1,), (1,)), ((), ())), preferred_element_type=F32)
            s = jnp.where(mask, s, NEG_BIG)
            m = jnp.max(s, axis=-1, keepdims=True)
            p = jnp.exp2((s - m) * (scale * LOG2_E))
            den = jnp.sum(p, axis=-1, keepdims=True)
            o = jnp.dot(p.astype(BF16), v, preferred_element_type=F32) / den
            o_ref[0, 0, pl.ds(r0, SPAN), cols] = o.astype(BF16)
            lse_tile = jnp.where(lane == hd, m * scale + jnp.log(den), lse_tile)
        l_ref[0, 0, pl.ds(r0, SPAN), :] = lse_tile
        return carry

    lax.fori_loop(0, nq, body, 0)


def _attn_group(qkv_g, group):
    batch, dilation, l, gw3 = qkv_g.shape
    gw = gw3 // 3
    qb_rows = min(512, l)
    sub = qb_rows // SPAN

    def cur(part):
        return pl.BlockSpec((1, 1, qb_rows, gw), lambda b, r, i: (b, r, i, part))

    def halo(part):
        return pl.BlockSpec((1, 1, SPAN, gw), lambda b, r, i: (b, r, jnp.maximum(i * sub - 1, 0), part))

    return pl.pallas_call(
        functools.partial(_attn_kernel, qb_rows=qb_rows),
        out_shape=(jax.ShapeDtypeStruct((batch, dilation, l, gw), BF16),
                   jax.ShapeDtypeStruct((batch, dilation, l, LANES), F32)),
        grid=(batch, dilation, l // qb_rows),
        in_specs=[cur(0), cur(1), cur(2), halo(1), halo(2)],
        out_specs=(pl.BlockSpec((1, 1, qb_rows, gw), lambda b, r, i: (b, r, i, 0)),
                   pl.BlockSpec((1, 1, qb_rows, LANES), lambda b, r, i: (b, r, i, 0))),
        scratch_shapes=[pltpu.VMEM((qb_rows + SPAN, gw), BF16), pltpu.VMEM((qb_rows + SPAN, gw), BF16)],
        compiler_params=_cparams(3, 40),
        name=f"dilated_attn_g{group}",
    )(qkv_g, qkv_g, qkv_g, qkv_g, qkv_g)


def _merge_kernel(*refs, dils):
    ng = len(dils)
    n_perm = sum(1 for dl in dils if dl > 1)
    o_refs, l_refs = refs[:ng], refs[ng:2 * ng]
    x_ref = refs[2 * ng]
    perm_refs = list(refs[2 * ng + 1:2 * ng + 1 + n_perm])
    (wo_ref, g1_ref, n2_ref, sh2_ref, sc2_ref, wr_ref, rb_ref,
     x1_ref, hp_ref, lg_ref) = refs[2 * ng + 1 + n_perm:]
    tm = x_ref.shape[0]
    os_, ls = [], []
    for g, dl in enumerate(dils):
        o = o_refs[g][0].reshape(tm, o_refs[g].shape[-1])
        l = l_refs[g][0].reshape(tm, LANES)
        if dl > 1:
            perm = perm_refs.pop(0)[...]
            o = jnp.dot(perm, o, preferred_element_type=F32)
            l_hi = l.astype(BF16)
            l_mid = (l - l_hi.astype(F32)).astype(BF16)
            l_lo = (l - l_hi.astype(F32) - l_mid.astype(F32)).astype(BF16)
            l = (jnp.dot(perm, l_hi, preferred_element_type=F32)
                 + jnp.dot(perm, l_mid, preferred_element_type=F32)
                 + jnp.dot(perm, l_lo, preferred_element_type=F32))
        os_.append(o)
        ls.append(l)
    m = functools.reduce(jnp.maximum, ls)
    es = [jnp.exp(l - m) for l in ls]
    den = functools.reduce(jnp.add, es)
    alphas = [e / den for e in es]
    parts = []
    for hd in range(HEADS):
        cols = slice(hd * HEAD_DIM, (hd + 1) * HEAD_DIM)
        parts.append(functools.reduce(
            jnp.add, [a[:, hd:hd + 1] * o[:, cols].astype(F32) for a, o in zip(alphas, os_)]))
    o = jnp.concatenate(parts, axis=-1).astype(BF16)
    y = jnp.dot(o, wo_ref[0], preferred_element_type=F32)
    x1 = x_ref[...] + g1_ref[0] * y
    x1_ref[...] = x1
    h2 = _norm_mod(x1, n2_ref[...], sc2_ref[0], sh2_ref[0])
    h_hi = h2.astype(BF16)
    h_hi32 = h_hi.astype(F32)
    h_lo = (h2 - h_hi32).astype(BF16)
    lg_ref[...] = (jnp.dot(h_hi, wr_ref[0], preferred_element_type=F32)
                   + jnp.dot(h_lo, wr_ref[0], preferred_element_type=F32)
                   + jnp.dot(h_hi, wr_ref[1], preferred_element_type=F32) + rb_ref[...])
    half = h2.shape[1] // 2
    lo = lax.bitcast_convert_type(h_hi32[:, :half], U32)
    hi = lax.bitcast_convert_type(h_hi32[:, half:], U32)
    hp_ref[...] = (lo >> 16) | (hi & jnp.uint32(0xFFFF0000))


def _merge_layer(os_, lses, x, mod, layer, batch, w_o, n2g, w_router, b_router):
    n, d = x.shape
    tm = min(512, n)
    per_b = (n // batch) // tm
    ne = w_router.shape[1]
    wr = jnp.zeros((d, LANES), F32).at[:, :ne].set(w_router)
    wr_hi = wr.astype(BF16)
    wr = jnp.stack([wr_hi, (wr - wr_hi.astype(F32)).astype(BF16)])
    rb = jnp.zeros((1, LANES), F32).at[0, :ne].set(b_router)

    def mspec(k):
        return pl.BlockSpec((1, 1, d), lambda i: ((layer * batch + i // per_b) * 6 + k, 0, 0))

    tile = lambda w: pl.BlockSpec((tm, w), lambda i: (i, 0))
    vec = lambda w: pl.BlockSpec((1, w), lambda i: (0, 0))
    dils = tuple(o.shape[1] for o in os_)
    planes = lambda dl, w: pl.BlockSpec((1, dl, tm // dl, w), lambda i: (i // per_b, 0, i % per_b, 0))
    unperm = [_residue_major_perm(tm, dl).T.astype(BF16) for dl in dils if dl > 1]
    return pl.pallas_call(
        functools.partial(_merge_kernel, dils=dils),
        out_shape=(jax.ShapeDtypeStruct((n, d), F32), jax.ShapeDtypeStruct((n, d // 2), U32),
                   jax.ShapeDtypeStruct((n, LANES), F32)),
        grid=(n // tm,),
        in_specs=[planes(dl, d) for dl in dils] + [planes(dl, LANES) for dl in dils] + [tile(d)]
                 + [_resident((tm, tm)) for _ in unperm]
                 + [_resident_slab((d, d), layer // 2), mspec(2), vec(d), mspec(3), mspec(4), _resident((2, d, LANES)), vec(LANES)],
        out_specs=(tile(d), tile(d // 2), tile(LANES)),
        compiler_params=_cparams(1, 48),
        name="attn_merge_proj",
    )(*os_, *lses, x, *unperm, w_o, mod, n2g.reshape(1, d), mod, mod, wr, rb)


def _route_kernel(lg_ref, route_ref, gate_ref, cnt_ref, carry_ref, *, tr):
    @pl.when(pl.program_id(0) == 0)
    def _():
        carry_ref[...] = jnp.zeros_like(carry_ref)

    lane = lax.broadcasted_iota(I32, (tr, LANES), 1)
    lg = jnp.where(lane < N_EXPERTS, lg_ref[...], NEG_BIG)
    m1 = jnp.max(lg, axis=-1, keepdims=True)
    i1 = jnp.min(jnp.where(lg == m1, lane, LANES), axis=-1, keepdims=True)
    lg2 = jnp.where(lane == i1, NEG_BIG, lg)
    m2 = jnp.max(lg2, axis=-1, keepdims=True)
    i2 = jnp.min(jnp.where(lg2 == m2, lane, LANES), axis=-1, keepdims=True)
    e = jnp.exp(m2 - m1)
    gate_ref[...] = jnp.where(lane == 0, 1.0 / (1.0 + e), jnp.where(lane == 1, e / (1.0 + e), 0.0))

    sel1 = lane == i1
    sel2 = lane == i2
    chosen = jnp.where(sel1 | sel2, 1.0, 0.0)
    r = lax.broadcasted_iota(I32, (tr, tr), 0)
    c = lax.broadcasted_iota(I32, (tr, tr), 1)
    earlier = jnp.where(c < r, 1.0, 0.0).astype(BF16)
    before = jnp.dot(earlier, chosen.astype(BF16), preferred_element_type=F32) + carry_ref[...]
    rank1 = jnp.sum(jnp.where(sel1, before, 0.0), axis=-1, keepdims=True).astype(I32)
    rank2 = jnp.sum(jnp.where(sel2, before, 0.0), axis=-1, keepdims=True).astype(I32)
    route_ref[...] = jnp.where(lane == 0, i1, jnp.where(lane == 1, i2,
                               jnp.where(lane == 2, rank1, jnp.where(lane == 3, rank2, 0))))
    total = carry_ref[...] + jnp.sum(chosen, axis=0, keepdims=True)
    carry_ref[...] = total
    cnt_ref[...] = total.astype(I32)


def _route(logits):
    n = logits.shape[0]
    tr = min(512, n)
    tile = pl.BlockSpec((tr, LANES), lambda i: (i, 0))
    return pl.pallas_call(
        functools.partial(_route_kernel, tr=tr),
        out_shape=(jax.ShapeDtypeStruct((n, LANES), I32), jax.ShapeDtypeStruct((n, LANES), F32),
                   jax.ShapeDtypeStruct((1, LANES), I32)),
        grid=(n // tr,),
        in_specs=[tile],
        out_specs=(tile, tile, pl.BlockSpec((1, LANES), lambda i: (0, 0))),
        scratch_shapes=[pltpu.VMEM((1, LANES), F32)],
        compiler_params=_cparams(1, 32),
        name="moe_route",
    )(logits)


ROW_DMA_GROUP = 8


def _row_dma_start(tm, pos_ref, row_copy):
    def start(c, carry):
        base = pl.multiple_of(c * ROW_DMA_GROUP, ROW_DMA_GROUP)
        where = [[pos_ref[0, slot, base + k] for slot in range(2)] for k in range(ROW_DMA_GROUP)]
        for k in range(ROW_DMA_GROUP):
            for slot in range(2):
                row_copy(base + k, slot, where[k][slot]).start(priority=slot)
        return carry

    lax.fori_loop(0, tm // ROW_DMA_GROUP, start, 0)


def _row_dma_wait(tm, row_copy):
    def wait(c, carry):
        base = pl.multiple_of(c * ROW_DMA_GROUP, ROW_DMA_GROUP)
        for k in range(ROW_DMA_GROUP):
            for slot in range(2):
                row_copy(base + k, slot, 0).wait()
        return carry

    lax.fori_loop(0, tm // ROW_DMA_GROUP, wait, 0)


def _dispatch_kernel(pos_ref, h_ref, xs_in_ref, xs_ref, sem, *, tm):
    del xs_in_ref

    def row_copy(t, slot, pos):
        del slot
        return pltpu.make_async_copy(h_ref.at[pl.ds(t, 1)], xs_ref.at[pl.ds(pos, 1)], sem)

    _row_dma_start(tm, pos_ref, row_copy)
    _row_dma_wait(tm, row_copy)


def _dispatch(h2p, pos_t, rows_padded, tm):
    n, hw = h2p.shape
    xs0 = jnp.zeros((rows_padded, hw), U32)
    return pl.pallas_call(
        functools.partial(_dispatch_kernel, tm=tm),
        out_shape=jax.ShapeDtypeStruct((rows_padded, hw), U32),
        grid=(n // tm,),
        in_specs=[
            pl.BlockSpec((1, 2, tm), lambda i: (i, 0, 0), memory_space=pltpu.SMEM),
            pl.BlockSpec((tm, hw), lambda i: (i, 0)),
            pl.BlockSpec(memory_space=pl.ANY),
        ],
        out_specs=pl.BlockSpec(memory_space=pl.ANY),
        scratch_shapes=[pltpu.SemaphoreType.DMA(())],
        input_output_aliases={2: 0},
        compiler_params=_cparams(1, 32),
        name="moe_dispatch",
    )(pos_t, h2p, xs0)


def _expert_kernel(te_ref, tv_ref, xs_ref, wg_ref, wu_ref, wd_ref, ys_ref, xb_ref, acc_ref, *, nfc, nsub):
    del te_ref
    i = pl.program_id(0)
    j = pl.program_id(1)

    @pl.when((tv_ref[i] == 0) & (j == nfc - 1))
    def _():
        ys_ref[...] = jnp.zeros_like(ys_ref)

    @pl.when(tv_ref[i] == 1)
    def _():
        @pl.when(j == 0)
        def _():
            w = xs_ref[...]
            half = w.shape[1]
            xb_ref[:, :half] = lax.bitcast_convert_type(w << 16, F32).astype(BF16)
            xb_ref[:, half:] = lax.bitcast_convert_type(w & jnp.uint32(0xFFFF0000), F32).astype(BF16)

        xb = xb_ref[...]
        sc = wg_ref.shape[2] // nsub
        part = None
        for s in range(nsub):
            g = jnp.dot(xb, wg_ref[0, :, s * sc:(s + 1) * sc], preferred_element_type=F32)
            u = jnp.dot(xb, wu_ref[0, :, s * sc:(s + 1) * sc], preferred_element_type=F32)
            a = (_silu(g) * u).astype(BF16)
            dn = jnp.dot(a, wd_ref[0, s * sc:(s + 1) * sc, :], preferred_element_type=F32)
            part = dn if part is None else part + dn

        if nfc == 1:
            ys_ref[...] = part
        else:
            @pl.when(j == 0)
            def _():
                acc_ref[...] = part

            if nfc > 2:
                @pl.when((j > 0) & (j < nfc - 1))
                def _():
                    acc_ref[...] += part

            @pl.when(j == nfc - 1)
            def _():
                ys_ref[...] = acc_ref[...] + part


def _expert_ffn(xs, tile_expert, tile_valid, w_gu, w_down, tm):
    rows, hw = xs.shape
    ne, d, two_ff = w_gu.shape
    ff = two_ff // 2
    fc = 1792 if ff % 1792 == 0 else ff
    nfc = ff // fc
    nsub = 2 if fc % (2 * LANES) == 0 else 1
    n_tiles = rows // tm

    def chunk(i, j, tv):
        return jnp.where(tv[i] == 1, j, nfc - 1)

    grid_spec = pltpu.PrefetchScalarGridSpec(
        num_scalar_prefetch=2,
        grid=(n_tiles, nfc),
        in_specs=[
            pl.BlockSpec((tm, hw), lambda i, j, te, tv: (i, 0)),
            pl.BlockSpec((1, d, fc), lambda i, j, te, tv: (te[i], 0, chunk(i, j, tv))),
            pl.BlockSpec((1, d, fc), lambda i, j, te, tv: (te[i], 0, nfc + chunk(i, j, tv))),
            pl.BlockSpec((1, fc, d), lambda i, j, te, tv: (te[i], chunk(i, j, tv), 0)),
        ],
        out_specs=pl.BlockSpec((tm, d), lambda i, j, te, tv: (i, 0)),
        scratch_shapes=[pltpu.VMEM((tm, d), BF16), pltpu.VMEM((tm, d), F32)],
    )
    return pl.pallas_call(
        functools.partial(_expert_kernel, nfc=nfc, nsub=nsub),
        out_shape=jax.ShapeDtypeStruct((rows, d), F32),
        grid_spec=grid_spec,
        compiler_params=_cparams(2, 56),
        name="moe_expert_swiglu",
    )(tile_expert, tile_valid, xs, w_gu, w_gu, w_down)


def _combine_kernel(pos_ref, pos_next_ref, gate_ref, x_ref, g2_ref, ys_ref, o_ref, ybuf, sems, *, tm):
    i = pl.program_id(0)
    cur = i % 2

    def row_copy_into(half):
        def row_copy(t, slot, pos):
            return pltpu.make_async_copy(ys_ref.at[pl.ds(pos, 1)], ybuf.at[half, slot, pl.ds(t, 1)], sems.at[half])
        return row_copy

    @pl.when(i == 0)
    def _():
        _row_dma_start(tm, pos_ref, row_copy_into(cur))

    @pl.when(i + 1 < pl.num_programs(0))
    def _():
        _row_dma_start(tm, pos_next_ref, row_copy_into(1 - cur))

    _row_dma_wait(tm, row_copy_into(cur))
    gate = gate_ref[...]
    y = gate[:, 0:1] * ybuf[cur, 0] + gate[:, 1:2] * ybuf[cur, 1]
    o_ref[...] = x_ref[...] + g2_ref[0] * y


def _combine(ys, pos_t, gates, x1, mod, layer, batch, tm):
    n, d = x1.shape
    per_b = (n // batch) // tm
    last = n // tm - 1
    return pl.pallas_call(
        functools.partial(_combine_kernel, tm=tm),
        out_shape=jax.ShapeDtypeStruct((n, d), F32),
        grid=(n // tm,),
        in_specs=[
            pl.BlockSpec((1, 2, tm), lambda i: (i, 0, 0), memory_space=pltpu.SMEM),
            pl.BlockSpec((1, 2, tm), lambda i: (jnp.minimum(i + 1, last), 0, 0), memory_space=pltpu.SMEM),
            pl.BlockSpec((tm, LANES), lambda i: (i, 0)),
            pl.BlockSpec((tm, d), lambda i: (i, 0)),
            pl.BlockSpec((1, 1, d), lambda i: ((layer * batch + i // per_b) * 6 + 5, 0, 0)),
            pl.BlockSpec(memory_space=pl.ANY),
        ],
        out_specs=pl.BlockSpec((tm, d), lambda i: (i, 0)),
        scratch_shapes=[pltpu.VMEM((2, 2, tm, d), F32), pltpu.SemaphoreType.DMA((2,))],
        compiler_params=_cparams(1, 40),
        name="moe_combine",
    )(pos_t, pos_t, gates, x1, mod, ys)


def _moe_layer(h2p, logits, x1, mod, layer, batch, w_gu, w_down):
    n = x1.shape[0]
    tm = min(512, n)
    tr = min(512, n)
    route, gates, counts = _route(logits)
    counts = counts[0, :N_EXPERTS]
    padded = ((counts + tm - 1) // tm) * tm
    ends = jnp.cumsum(padded)
    offs = (ends - padded).astype(I32)
    n_tiles = (2 * n) // tm + N_EXPERTS
    rows_padded = n_tiles * tm
    tile_start = jnp.arange(n_tiles, dtype=I32) * tm
    tile_valid = (tile_start < ends[-1]).astype(I32)
    tile_expert = jnp.minimum(jnp.sum(tile_start[:, None] >= ends[None, :], axis=1), N_EXPERTS - 1).astype(I32)
    tile_expert = tile_expert + (layer // 2) * N_EXPERTS
    pos = offs[route[:, 0:2]] + route[:, 2:4]
    pos_t = pos.reshape(n // tr, tr, 2).transpose(0, 2, 1)
    xs = _dispatch(h2p, pos_t, rows_padded, tr)
    ys = _expert_ffn(xs, tile_expert, tile_valid, w_gu, w_down, tm)
    return _combine(ys, pos_t, gates, x1, mod, layer, batch, tr)


def kernel(x, c, positions, norm1_g, norm2_g, ada_w, ada_b, pool_w, pool_scale, attn_w_qkv, attn_w_o,
           q_norm_g, k_norm_g, ffn_w_gu, ffn_w_down, router_w, router_b, moe_w_gu, moe_w_down):
    batch, seq, d = x.shape
    depth = ada_w.shape[0]
    n = batch * seq
    mod = _ada_modulation(c, ada_w, ada_b).reshape(depth * batch * 6, 1, d)
    cs, sn = _rope_tables(positions)
    pool_wb, ffn_gub, ffn_db = pool_w.astype(BF16), ffn_w_gu.astype(BF16), ffn_w_down.astype(BF16)
    n_groups = len(DIL_PATTERNS)
    w6 = attn_w_qkv.reshape(attn_w_qkv.shape[:2] + (n_groups, 3, HEADS, HEAD_DIM))
    w6 = jnp.concatenate([_rotary_head_layout(w6[:, :, :, :2]), w6[:, :, :, 2:]], axis=3)
    qkv_wb = w6.astype(BF16).reshape(attn_w_qkv.shape)
    q_gain, k_gain = _rotary_head_layout(q_norm_g), _rotary_head_layout(k_norm_g)
    wo_b = attn_w_o.astype(BF16)
    moe_gub = moe_w_gu.astype(BF16).reshape((-1,) + moe_w_gu.shape[2:])
    moe_db = moe_w_down.astype(BF16).reshape((-1,) + moe_w_down.shape[2:])
    xf = x.reshape(n, d)
    for layer in range(depth):
        j = layer // 2
        if layer % 2 == 0:
            x1, h2 = _pool_layer(xf.reshape(batch, seq, d), mod, layer, norm1_g[layer], norm2_g[layer],
                                 pool_wb, pool_scale[j])
            xf = _ffn_layer(h2.reshape(n, d), x1.reshape(n, d), mod, layer, batch, ffn_gub, ffn_db)
        else:
            qkvs = _qkv_layer(xf.reshape(batch, seq, d), mod, layer, norm1_g[layer],
                              qkv_wb, q_gain[j], k_gain[j], cs, sn)
            outs = [_attn_group(qkv_g, g) for g, qkv_g in enumerate(qkvs)]
            x1, h2p, logits = _merge_layer([o for o, _ in outs], [l for _, l in outs], xf, mod, layer, batch,
                                           wo_b, norm2_g[layer], router_w[j], router_b[j])
            xf = _moe_layer(h2p, logits, x1, mod, layer, batch, moe_gub, moe_db)
    return xf.reshape(batch, seq, d)
```

```python
import functools

import jax
import jax.numpy as jnp
from jax import lax
from jax.experimental import pallas as pl
from jax.experimental.pallas import tpu as pltpu

F32 = jnp.float32
BF16 = jnp.bfloat16
U32 = jnp.uint32
I32 = jnp.int32

POOL_WINDOWS = (2, 4, 8, 16)
POOL_HALO = 16
DIL_PATTERNS = ((128, 1), (512, 4), (2048, 16))
HEADS = 8
HEAD_DIM = 128
ROPE_DIM = 32
ROPE_THETA = 500000.0
N_EXPERTS = 8
NORM_EPS = 1e-6
NEG_BIG = -1e30

LANES = 128
SPAN = 128
MIB = 1024 * 1024
HIGHEST = lax.Precision.HIGHEST


def _cparams(n_axes, vmem_mib):
    return pltpu.CompilerParams(
        dimension_semantics=("arbitrary",) * n_axes,
        vmem_limit_bytes=int(vmem_mib * MIB),
    )


def _silu(v):
    return v / (1.0 + jnp.exp(-v))


def _norm_mod(t, gain, scale, shift):
    ms = jnp.mean(t * t, axis=-1, keepdims=True)
    return t * lax.rsqrt(ms + NORM_EPS) * gain * (1.0 + scale) + shift


def _resident(shape):
    nd = len(shape)
    return pl.BlockSpec(shape, lambda *_: (0,) * nd, pipeline_mode=pl.Buffered(1))


def _resident_slab(shape, j):
    nd = len(shape)
    return pl.BlockSpec((1,) + tuple(shape), lambda *_: (j,) + (0,) * nd, pipeline_mode=pl.Buffered(1))


def _ada_kernel(c_ref, w_ref, b_ref, o_ref):
    ca = _silu(c_ref[...])
    o_ref[0] = jnp.dot(ca, w_ref[0], precision=HIGHEST, preferred_element_type=F32) + b_ref[0]


def _ada_modulation(c, ada_w, ada_b):
    depth, d, six_d = ada_w.shape
    b = c.shape[0]
    tn = 1536
    return pl.pallas_call(
        _ada_kernel,
        out_shape=jax.ShapeDtypeStruct((depth, b, six_d), F32),
        grid=(depth, six_d // tn),
        in_specs=[
            pl.BlockSpec((b, d), lambda l, j: (0, 0)),
            pl.BlockSpec((1, d, tn), lambda l, j: (l, 0, j)),
            pl.BlockSpec((1, 1, tn), lambda l, j: (l, 0, j)),
        ],
        out_specs=pl.BlockSpec((1, b, tn), lambda l, j: (l, 0, j)),
        compiler_params=_cparams(2, 32),
        name="ada_modulation",
    )(c, ada_w, ada_b.reshape(depth, 1, six_d))


def _rope_kernel(pos_ref, invf_ref, cs_ref, sn_ref):
    ang = pos_ref[...].astype(F32) * invf_ref[...]
    lane = lax.broadcasted_iota(I32, ang.shape, 1)
    half = ROPE_DIM // 2
    c = jnp.cos(ang)
    s = jnp.sin(ang)
    cs_ref[...] = jnp.where(lane < ROPE_DIM, c, 1.0)
    sn_ref[...] = jnp.where(lane < half, -s, jnp.where(lane < ROPE_DIM, s, 0.0))


def _rope_tables(positions):
    n = positions.size
    half = ROPE_DIM // 2
    inv_freq = ROPE_THETA ** (-jnp.arange(half, dtype=F32) * 2.0 / ROPE_DIM)
    invf = jnp.concatenate([inv_freq, inv_freq, jnp.zeros((LANES - ROPE_DIM,), F32)]).reshape(1, LANES)
    tm = min(1024, n)
    return pl.pallas_call(
        _rope_kernel,
        out_shape=(jax.ShapeDtypeStruct((n, LANES), F32),) * 2,
        grid=(n // tm,),
        in_specs=[pl.BlockSpec((tm, 1), lambda i: (i, 0)), pl.BlockSpec((1, LANES), lambda i: (0, 0))],
        out_specs=(pl.BlockSpec((tm, LANES), lambda i: (i, 0)),) * 2,
        compiler_params=_cparams(1, 32),
        name="rope_tables",
    )(positions.reshape(n, 1), invf)


def _pool_kernel(xc_ref, xh_ref, n1_ref, sh1_ref, sc1_ref, g1_ref, pw_ref, ps_ref,
                 n2_ref, sh2_ref, sc2_ref, x1_ref, h2_ref, *, ts):
    i = pl.program_id(1)
    xc = xc_ref[0]
    hc = _norm_mod(xc, n1_ref[...], sc1_ref[0], sh1_ref[0])
    hh = _norm_mod(xh_ref[0], n1_ref[...], sc1_ref[0], sh1_ref[0])
    hh = jnp.where(i > 0, hh, 0.0)
    hcat = jnp.concatenate([hh, hc], axis=0)
    pos = i * ts + lax.broadcasted_iota(I32, (ts, 1), 0)
    gd = hc.shape[1] // len(POOL_WINDOWS)
    ys = []
    for gi, w in enumerate(POOL_WINDOWS):
        s = hcat[:, gi * gd:(gi + 1) * gd]
        span = 1
        while span < w:
            s = s + pltpu.roll(s, span, 0)
            span *= 2
        win = s[POOL_HALO:, :]
        cnt = jnp.minimum(pos + 1, w).astype(F32)
        pooled = win / cnt - hc[:, gi * gd:(gi + 1) * gd]
        ys.append(jnp.dot(pooled.astype(BF16), pw_ref[0, gi], preferred_element_type=F32))
    y = jnp.concatenate(ys, axis=-1) * ps_ref[...]
    x1 = xc + g1_ref[0] * y
    x1_ref[0] = x1
    h2_ref[0] = _norm_mod(x1, n2_ref[...], sc2_ref[0], sh2_ref[0]).astype(BF16)


def _pool_layer(x3, mod, layer, n1g, n2g, pool_w, pool_scale):
    b, s, d = x3.shape
    ts = min(512, s)
    hb = ts // POOL_HALO
    _, ng, gd, _ = pool_w.shape

    def mspec(k):
        return pl.BlockSpec((1, 1, d), lambda bi, i: ((layer * b + bi) * 6 + k, 0, 0))

    vec = pl.BlockSpec((1, d), lambda bi, i: (0, 0))
    tile = pl.BlockSpec((1, ts, d), lambda bi, i: (bi, i, 0))
    return pl.pallas_call(
        functools.partial(_pool_kernel, ts=ts),
        out_shape=(jax.ShapeDtypeStruct((b, s, d), F32), jax.ShapeDtypeStruct((b, s, d), BF16)),
        grid=(b, s // ts),
        in_specs=[
            tile,
            pl.BlockSpec((1, POOL_HALO, d), lambda bi, i: (bi, jnp.maximum(i * hb - 1, 0), 0)),
            vec, mspec(0), mspec(1), mspec(2),
            pl.BlockSpec((1, ng, gd, gd), lambda bi, i: (layer // 2, 0, 0, 0)),
            vec, vec, mspec(3), mspec(4),
        ],
        out_specs=(tile, tile),
        compiler_params=_cparams(2, 48),
        name="pool_mixer",
    )(x3, x3, n1g.reshape(1, d), mod, mod, mod, pool_w, pool_scale.reshape(1, d),
      n2g.reshape(1, d), mod, mod)


def _ffn_kernel(h_ref, x_ref, g2_ref, wgu_ref, wd_ref, o_ref, *, ff, fc):
    h = h_ref[...]
    acc = jnp.zeros(x_ref.shape, F32)
    for c in range(ff // fc):
        g = jnp.dot(h, wgu_ref[0, :, c * fc:(c + 1) * fc], preferred_element_type=F32)
        u = jnp.dot(h, wgu_ref[0, :, ff + c * fc:ff + (c + 1) * fc], preferred_element_type=F32)
        a = (_silu(g) * u).astype(BF16)
        acc = acc + jnp.dot(a, wd_ref[0, c * fc:(c + 1) * fc, :], preferred_element_type=F32)
    o_ref[...] = x_ref[...] + g2_ref[0] * acc


def _ffn_layer(h2, x1, mod, layer, batch, w_gu, w_down):
    n, d = x1.shape
    ff = w_down.shape[1]
    tm = min(512, n)
    per_b = (n // batch) // tm
    fc = ff // 2 if (ff // 2) % LANES == 0 else ff
    tile = lambda: pl.BlockSpec((tm, d), lambda i: (i, 0))
    return pl.pallas_call(
        functools.partial(_ffn_kernel, ff=ff, fc=fc),
        out_shape=jax.ShapeDtypeStruct((n, d), F32),
        grid=(n // tm,),
        in_specs=[
            tile(), tile(),
            pl.BlockSpec((1, 1, d), lambda i: ((layer * batch + i // per_b) * 6 + 5, 0, 0)),
            _resident_slab((d, 2 * ff), layer // 2), _resident_slab((ff, d), layer // 2),
        ],
        out_specs=tile(),
        compiler_params=_cparams(1, 56),
        name="dense_swiglu",
    )(h2, x1, mod, w_gu, w_down)


def _residue_major_perm(tm, dilation):
    per = tm // dilation
    p = jnp.arange(tm)
    src = (p % per) * dilation + p // per
    return (src[:, None] == jnp.arange(tm)[None, :]).astype(F32)


def _qkv_kernel(x_ref, n1_ref, sh1_ref, sc1_ref, w_ref, qg_ref, kg_ref, cs_ref, sn_ref, *rest, dils):
    out_refs = rest[:len(dils)]
    hs_ref = rest[-1]
    h32 = _norm_mod(x_ref[0], n1_ref[...], sc1_ref[0], sh1_ref[0])
    tm, d = h32.shape
    n_slabs = d // LANES
    for c in range(n_slabs):
        hs_ref[c] = h32[:, c * LANES:(c + 1) * LANES]
    first_half = lax.broadcasted_iota(I32, (tm, LANES), 1) < ROPE_DIM // 2
    gw = HEADS * HEAD_DIM
    hs = []
    for g, dl in enumerate(dils):
        per = tm // dl

        def residue_major(ref_2d):
            return jnp.concatenate([ref_2d[pl.ds(r, per, stride=dl), :] for r in range(dl)], axis=0)

        if dl == 1:
            h, cs, sn = h32.astype(BF16), cs_ref[...], sn_ref[...]
        else:
            h = jnp.concatenate([residue_major(hs_ref.at[c]).astype(BF16) for c in range(n_slabs)], axis=1)
            cs, sn = residue_major(cs_ref), residue_major(sn_ref)
        hs.append(h)
        out = out_refs[g]
        for part in range(2):
            blk = g * 3 + part
            y = jnp.dot(h, w_ref[0, :, blk * gw:(blk + 1) * gw], preferred_element_type=F32)
            gain = (qg_ref if part == 0 else kg_ref)[g:g + 1, :]
            for hd in range(HEADS):
                yh = y[:, hd * HEAD_DIM:(hd + 1) * HEAD_DIM]
                ms = jnp.mean(yh * yh, axis=-1, keepdims=True)
                yn = yh * lax.rsqrt(ms + NORM_EPS) * gain
                partner = jnp.where(first_half, pltpu.roll(yn, LANES - ROPE_DIM // 2, 1),
                                    pltpu.roll(yn, ROPE_DIM // 2, 1))
                c0 = part * gw + hd * HEAD_DIM
                out[0, :, :, c0:c0 + HEAD_DIM] = (yn * cs + partner * sn).astype(BF16).reshape(dl, per, HEAD_DIM)
    for g, dl in enumerate(dils):
        blk = g * 3 + 2
        y = jnp.dot(hs[g], w_ref[0, :, blk * gw:(blk + 1) * gw], preferred_element_type=F32)
        out_refs[g][0, :, :, 2 * gw:3 * gw] = y.astype(BF16).reshape(dl, tm // dl, gw)


def _qkv_layer(x3, mod, layer, n1g, w_qkv, qg, kg, cs, sn):
    batch, s, d = x3.shape
    width = w_qkv.shape[2]
    tm = min(512, s)
    per_b = s // tm
    dils = tuple(dl for _, dl in DIL_PATTERNS)
    gw3 = width // len(dils)

    def mspec(k):
        return pl.BlockSpec((1, 1, d), lambda b, i: ((layer * batch + b) * 6 + k, 0, 0))

    small = lambda a: pl.BlockSpec(a.shape, lambda b, i: (0, 0))
    table = pl.BlockSpec((tm, LANES), lambda b, i: (b * per_b + i, 0))
    return pl.pallas_call(
        functools.partial(_qkv_kernel, dils=dils),
        out_shape=tuple(jax.ShapeDtypeStruct((batch, dl, s // dl, gw3), BF16) for dl in dils),
        grid=(batch, per_b),
        in_specs=[
            pl.BlockSpec((1, tm, d), lambda b, i: (b, i, 0)),
            pl.BlockSpec((1, d), lambda b, i: (0, 0)), mspec(0), mspec(1),
            _resident_slab((d, width), layer // 2), small(qg), small(kg), table, table,
        ],
        out_specs=tuple(pl.BlockSpec((1, dl, tm // dl, gw3), lambda b, i: (b, 0, i, 0)) for dl in dils),
        scratch_shapes=[pltpu.VMEM((d // LANES, tm, LANES), F32)],
        compiler_params=_cparams(2, 60),
        name="qkv_proj",
    )(x3, n1g.reshape(1, d), mod, mod, w_qkv, qg, kg, cs, sn)


def _attn_kernel(q_ref, kc_ref, vc_ref, kh_ref, vh_ref, o_ref, l_ref, kx_ref, vx_ref, *, qb_rows):
    n = pl.program_id(2)
    nq = qb_rows // SPAN
    kx_ref[0:SPAN, :] = kh_ref[0, 0]
    kx_ref[SPAN:, :] = kc_ref[0, 0]
    vx_ref[0:SPAN, :] = vh_ref[0, 0]
    vx_ref[SPAN:, :] = vc_ref[0, 0]
    row = lax.broadcasted_iota(I32, (SPAN, 2 * SPAN), 0)
    col = lax.broadcasted_iota(I32, (SPAN, 2 * SPAN), 1)
    band = (col >= row) & (col <= row + SPAN)
    lane = lax.broadcasted_iota(I32, (SPAN, LANES), 1)
    scale = HEAD_DIM ** -0.5

    def body(qb, carry):
        r0 = pl.multiple_of(qb * SPAN, SPAN)
        has_prev = (n * nq + qb) > 0
        mask = band & ((col >= SPAN) | has_prev)
        lse_tile = jnp.zeros((SPAN, LANES), F32)
        for hd in range(HEADS):
            cols = slice(hd * HEAD_DIM, (hd + 1) * HEAD_DIM)
            q = q_ref[0, 0, pl.ds(r0, SPAN), cols]
            k = kx_ref[pl.ds(r0, 2 * SPAN), cols]
            v = vx_ref[pl.ds(r0, 2 * SPAN), cols]
            s = lax.dot_general(q, k, (((1,), (1,)), ((), ())), preferred_element_type=F32) * scale
            s = jnp.where(mask, s, NEG_BIG)
            m = jnp.max(s, axis=-1, keepdims=True)
            p = jnp.exp(s - m)
            den = jnp.sum(p, axis=-1, keepdims=True)
            o = jnp.dot(p.astype(BF16), v, preferred_element_type=F32) / den
            o_ref[0, 0, pl.ds(r0, SPAN), cols] = o.astype(BF16)
            lse_tile = jnp.where(lane == hd, m + jnp.log(den), lse_tile)
        l_ref[0, 0, pl.ds(r0, SPAN), :] = lse_tile
        return carry

    lax.fori_loop(0, nq, body, 0)


def _attn_group(qkv_g, group):
    batch, dilation, l, gw3 = qkv_g.shape
    gw = gw3 // 3
    qb_rows = min(512, l)
    sub = qb_rows // SPAN

    def cur(part):
        return pl.BlockSpec((1, 1, qb_rows, gw), lambda b, r, i: (b, r, i, part))

    def halo(part):
        return pl.BlockSpec((1, 1, SPAN, gw), lambda b, r, i: (b, r, jnp.maximum(i * sub - 1, 0), part))

    return pl.pallas_call(
        functools.partial(_attn_kernel, qb_rows=qb_rows),
        out_shape=(jax.ShapeDtypeStruct((batch, dilation, l, gw), BF16),
                   jax.ShapeDtypeStruct((batch, dilation, l, LANES), F32)),
        grid=(batch, dilation, l // qb_rows),
        in_specs=[cur(0), cur(1), cur(2), halo(1), halo(2)],
        out_specs=(pl.BlockSpec((1, 1, qb_rows, gw), lambda b, r, i: (b, r, i, 0)),
                   pl.BlockSpec((1, 1, qb_rows, LANES), lambda b, r, i: (b, r, i, 0))),
        scratch_shapes=[pltpu.VMEM((qb_rows + SPAN, gw), BF16), pltpu.VMEM((qb_rows + SPAN, gw), BF16)],
        compiler_params=_cparams(3, 40),
        name=f"dilated_attn_g{group}",
    )(qkv_g, qkv_g, qkv_g, qkv_g, qkv_g)


def _merge_kernel(*refs, dils):
    ng = len(dils)
    n_perm = sum(1 for dl in dils if dl > 1)
    o_refs, l_refs = refs[:ng], refs[ng:2 * ng]
    x_ref = refs[2 * ng]
    perm_refs = list(refs[2 * ng + 1:2 * ng + 1 + n_perm])
    (wo_ref, g1_ref, n2_ref, sh2_ref, sc2_ref, wr_ref, rb_ref,
     x1_ref, hp_ref, lg_ref) = refs[2 * ng + 1 + n_perm:]
    tm = x_ref.shape[0]
    os_, ls = [], []
    for g, dl in enumerate(dils):
        o = o_refs[g][0].reshape(tm, o_refs[g].shape[-1])
        l = l_refs[g][0].reshape(tm, LANES)
        if dl > 1:
            perm = perm_refs.pop(0)[...]
            o = jnp.dot(perm, o, preferred_element_type=F32)
            l_hi = l.astype(BF16)
            l_mid = (l - l_hi.astype(F32)).astype(BF16)
            l_lo = (l - l_hi.astype(F32) - l_mid.astype(F32)).astype(BF16)
            l = (jnp.dot(perm, l_hi, preferred_element_type=F32)
                 + jnp.dot(perm, l_mid, preferred_element_type=F32)
                 + jnp.dot(perm, l_lo, preferred_element_type=F32))
        os_.append(o)
        ls.append(l)
    m = functools.reduce(jnp.maximum, ls)
    es = [jnp.exp(l - m) for l in ls]
    den = functools.reduce(jnp.add, es)
    alphas = [e / den for e in es]
    parts = []
    for hd in range(HEADS):
        cols = slice(hd * HEAD_DIM, (hd + 1) * HEAD_DIM)
        parts.append(functools.reduce(
            jnp.add, [a[:, hd:hd + 1] * o[:, cols].astype(F32) for a, o in zip(alphas, os_)]))
    o = jnp.concatenate(parts, axis=-1).astype(BF16)
    y = jnp.dot(o, wo_ref[0], preferred_element_type=F32)
    x1 = x_ref[...] + g1_ref[0] * y
    x1_ref[...] = x1
    h2 = _norm_mod(x1, n2_ref[...], sc2_ref[0], sh2_ref[0])
    h_hi = h2.astype(BF16)
    h_hi32 = h_hi.astype(F32)
    h_lo = (h2 - h_hi32).astype(BF16)
    lg_ref[...] = (jnp.dot(h_hi, wr_ref[0], preferred_element_type=F32)
                   + jnp.dot(h_lo, wr_ref[0], preferred_element_type=F32)
                   + jnp.dot(h_hi, wr_ref[1], preferred_element_type=F32) + rb_ref[...])
    half = h2.shape[1] // 2
    lo = lax.bitcast_convert_type(h_hi32[:, :half], U32)
    hi = lax.bitcast_convert_type(h_hi32[:, half:], U32)
    hp_ref[...] = (lo >> 16) | (hi & jnp.uint32(0xFFFF0000))


def _merge_layer(os_, lses, x, mod, layer, batch, w_o, n2g, w_router, b_router):
    n, d = x.shape
    tm = min(512, n)
    per_b = (n // batch) // tm
    ne = w_router.shape[1]
    wr = jnp.zeros((d, LANES), F32).at[:, :ne].set(w_router)
    wr_hi = wr.astype(BF16)
    wr = jnp.stack([wr_hi, (wr - wr_hi.astype(F32)).astype(BF16)])
    rb = jnp.zeros((1, LANES), F32).at[0, :ne].set(b_router)

    def mspec(k):
        return pl.BlockSpec((1, 1, d), lambda i: ((layer * batch + i // per_b) * 6 + k, 0, 0))

    tile = lambda w: pl.BlockSpec((tm, w), lambda i: (i, 0))
    vec = lambda w: pl.BlockSpec((1, w), lambda i: (0, 0))
    dils = tuple(o.shape[1] for o in os_)
    planes = lambda dl, w: pl.BlockSpec((1, dl, tm // dl, w), lambda i: (i // per_b, 0, i % per_b, 0))
    unperm = [_residue_major_perm(tm, dl).T.astype(BF16) for dl in dils if dl > 1]
    return pl.pallas_call(
        functools.partial(_merge_kernel, dils=dils),
        out_shape=(jax.ShapeDtypeStruct((n, d), F32), jax.ShapeDtypeStruct((n, d // 2), U32),
                   jax.ShapeDtypeStruct((n, LANES), F32)),
        grid=(n // tm,),
        in_specs=[planes(dl, d) for dl in dils] + [planes(dl, LANES) for dl in dils] + [tile(d)]
                 + [_resident((tm, tm)) for _ in unperm]
                 + [_resident_slab((d, d), layer // 2), mspec(2), vec(d), mspec(3), mspec(4), _resident((2, d, LANES)), vec(LANES)],
        out_specs=(tile(d), tile(d // 2), tile(LANES)),
        compiler_params=_cparams(1, 48),
        name="attn_merge_proj",
    )(*os_, *lses, x, *unperm, w_o, mod, n2g.reshape(1, d), mod, mod, wr, rb)


def _route_kernel(lg_ref, route_ref, gate_ref, cnt_ref, carry_ref, *, tr):
    @pl.when(pl.program_id(0) == 0)
    def _():
        carry_ref[...] = jnp.zeros_like(carry_ref)

    lane = lax.broadcasted_iota(I32, (tr, LANES), 1)
    lg = jnp.where(lane < N_EXPERTS, lg_ref[...], NEG_BIG)
    m1 = jnp.max(lg, axis=-1, keepdims=True)
    i1 = jnp.min(jnp.where(lg == m1, lane, LANES), axis=-1, keepdims=True)
    lg2 = jnp.where(lane == i1, NEG_BIG, lg)
    m2 = jnp.max(lg2, axis=-1, keepdims=True)
    i2 = jnp.min(jnp.where(lg2 == m2, lane, LANES), axis=-1, keepdims=True)
    e = jnp.exp(m2 - m1)
    gate_ref[...] = jnp.where(lane == 0, 1.0 / (1.0 + e), jnp.where(lane == 1, e / (1.0 + e), 0.0))

    sel1 = lane == i1
    sel2 = lane == i2
    chosen = jnp.where(sel1 | sel2, 1.0, 0.0)
    r = lax.broadcasted_iota(I32, (tr, tr), 0)
    c = lax.broadcasted_iota(I32, (tr, tr), 1)
    earlier = jnp.where(c < r, 1.0, 0.0).astype(BF16)
    before = jnp.dot(earlier, chosen.astype(BF16), preferred_element_type=F32) + carry_ref[...]
    rank1 = jnp.sum(jnp.where(sel1, before, 0.0), axis=-1, keepdims=True).astype(I32)
    rank2 = jnp.sum(jnp.where(sel2, before, 0.0), axis=-1, keepdims=True).astype(I32)
    route_ref[...] = jnp.where(lane == 0, i1, jnp.where(lane == 1, i2,
                               jnp.where(lane == 2, rank1, jnp.where(lane == 3, rank2, 0))))
    total = carry_ref[...] + jnp.sum(chosen, axis=0, keepdims=True)
    carry_ref[...] = total
    cnt_ref[...] = total.astype(I32)


def _route(logits):
    n = logits.shape[0]
    tr = min(512, n)
    tile = pl.BlockSpec((tr, LANES), lambda i: (i, 0))
    return pl.pallas_call(
        functools.partial(_route_kernel, tr=tr),
        out_shape=(jax.ShapeDtypeStruct((n, LANES), I32), jax.ShapeDtypeStruct((n, LANES), F32),
                   jax.ShapeDtypeStruct((1, LANES), I32)),
        grid=(n // tr,),
        in_specs=[tile],
        out_specs=(tile, tile, pl.BlockSpec((1, LANES), lambda i: (0, 0))),
        scratch_shapes=[pltpu.VMEM((1, LANES), F32)],
        compiler_params=_cparams(1, 32),
        name="moe_route",
    )(logits)


ROW_DMA_GROUP = 8


def _row_dma_loop(tm, pos_ref, row_copy):
    def start(c, carry):
        base = pl.multiple_of(c * ROW_DMA_GROUP, ROW_DMA_GROUP)
        where = [[pos_ref[0, slot, base + k] for slot in range(2)] for k in range(ROW_DMA_GROUP)]
        for k in range(ROW_DMA_GROUP):
            for slot in range(2):
                row_copy(base + k, slot, where[k][slot]).start(priority=slot)
        return carry

    def wait(c, carry):
        base = pl.multiple_of(c * ROW_DMA_GROUP, ROW_DMA_GROUP)
        for k in range(ROW_DMA_GROUP):
            for slot in range(2):
                row_copy(base + k, slot, 0).wait()
        return carry

    lax.fori_loop(0, tm // ROW_DMA_GROUP, start, 0)
    lax.fori_loop(0, tm // ROW_DMA_GROUP, wait, 0)


def _dispatch_kernel(pos_ref, h_ref, xs_in_ref, xs_ref, sem, *, tm):
    del xs_in_ref

    def row_copy(t, slot, pos):
        del slot
        return pltpu.make_async_copy(h_ref.at[pl.ds(t, 1)], xs_ref.at[pl.ds(pos, 1)], sem)

    _row_dma_loop(tm, pos_ref, row_copy)


def _dispatch(h2p, pos_t, rows_padded, tm):
    n, hw = h2p.shape
    xs0 = jnp.zeros((rows_padded, hw), U32)
    return pl.pallas_call(
        functools.partial(_dispatch_kernel, tm=tm),
        out_shape=jax.ShapeDtypeStruct((rows_padded, hw), U32),
        grid=(n // tm,),
        in_specs=[
            pl.BlockSpec((1, 2, tm), lambda i: (i, 0, 0), memory_space=pltpu.SMEM),
            pl.BlockSpec((tm, hw), lambda i: (i, 0)),
            pl.BlockSpec(memory_space=pl.ANY),
        ],
        out_specs=pl.BlockSpec(memory_space=pl.ANY),
        scratch_shapes=[pltpu.SemaphoreType.DMA(())],
        input_output_aliases={2: 0},
        compiler_params=_cparams(1, 32),
        name="moe_dispatch",
    )(pos_t, h2p, xs0)


def _expert_kernel(te_ref, tv_ref, xs_ref, wg_ref, wu_ref, wd_ref, ys_ref, xb_ref, acc_ref, *, nfc, nsub):
    del te_ref
    i = pl.program_id(0)
    j = pl.program_id(1)

    @pl.when((tv_ref[i] == 0) & (j == nfc - 1))
    def _():
        ys_ref[...] = jnp.zeros_like(ys_ref)

    @pl.when(tv_ref[i] == 1)
    def _():
        @pl.when(j == 0)
        def _():
            w = xs_ref[...]
            half = w.shape[1]
            xb_ref[:, :half] = lax.bitcast_convert_type(w << 16, F32).astype(BF16)
            xb_ref[:, half:] = lax.bitcast_convert_type(w & jnp.uint32(0xFFFF0000), F32).astype(BF16)

        xb = xb_ref[...]
        sc = wg_ref.shape[2] // nsub
        part = None
        for s in range(nsub):
            g = jnp.dot(xb, wg_ref[0, :, s * sc:(s + 1) * sc], preferred_element_type=F32)
            u = jnp.dot(xb, wu_ref[0, :, s * sc:(s + 1) * sc], preferred_element_type=F32)
            a = (_silu(g) * u).astype(BF16)
            dn = jnp.dot(a, wd_ref[0, s * sc:(s + 1) * sc, :], preferred_element_type=F32)
            part = dn if part is None else part + dn

        if nfc == 1:
            ys_ref[...] = part
        else:
            @pl.when(j == 0)
            def _():
                acc_ref[...] = part

            if nfc > 2:
                @pl.when((j > 0) & (j < nfc - 1))
                def _():
                    acc_ref[...] += part

            @pl.when(j == nfc - 1)
            def _():
                ys_ref[...] = acc_ref[...] + part


def _expert_ffn(xs, tile_expert, tile_valid, w_gu, w_down, tm):
    rows, hw = xs.shape
    ne, d, two_ff = w_gu.shape
    ff = two_ff // 2
    fc = 1792 if ff % 1792 == 0 else ff
    nfc = ff // fc
    nsub = 2 if fc % (2 * LANES) == 0 else 1
    n_tiles = rows // tm

    def chunk(i, j, tv):
        return jnp.where(tv[i] == 1, j, nfc - 1)

    grid_spec = pltpu.PrefetchScalarGridSpec(
        num_scalar_prefetch=2,
        grid=(n_tiles, nfc),
        in_specs=[
            pl.BlockSpec((tm, hw), lambda i, j, te, tv: (i, 0)),
            pl.BlockSpec((1, d, fc), lambda i, j, te, tv: (te[i], 0, chunk(i, j, tv))),
            pl.BlockSpec((1, d, fc), lambda i, j, te, tv: (te[i], 0, nfc + chunk(i, j, tv))),
            pl.BlockSpec((1, fc, d), lambda i, j, te, tv: (te[i], chunk(i, j, tv), 0)),
        ],
        out_specs=pl.BlockSpec((tm, d), lambda i, j, te, tv: (i, 0)),
        scratch_shapes=[pltpu.VMEM((tm, d), BF16), pltpu.VMEM((tm, d), F32)],
    )
    return pl.pallas_call(
        functools.partial(_expert_kernel, nfc=nfc, nsub=nsub),
        out_shape=jax.ShapeDtypeStruct((rows, d), F32),
        grid_spec=grid_spec,
        compiler_params=_cparams(2, 56),
        name="moe_expert_swiglu",
    )(tile_expert, tile_valid, xs, w_gu, w_gu, w_down)


def _combine_kernel(pos_ref, gate_ref, x_ref, g2_ref, ys_ref, o_ref, ybuf, sem, *, tm):
    def row_copy(t, slot, pos):
        return pltpu.make_async_copy(ys_ref.at[pl.ds(pos, 1)], ybuf.at[slot, pl.ds(t, 1)], sem)

    _row_dma_loop(tm, pos_ref, row_copy)
    gate = gate_ref[...]
    y = gate[:, 0:1] * ybuf[0] + gate[:, 1:2] * ybuf[1]
    o_ref[...] = x_ref[...] + g2_ref[0] * y


def _combine(ys, pos_t, gates, x1, mod, layer, batch, tm):
    n, d = x1.shape
    per_b = (n // batch) // tm
    return pl.pallas_call(
        functools.partial(_combine_kernel, tm=tm),
        out_shape=jax.ShapeDtypeStruct((n, d), F32),
        grid=(n // tm,),
        in_specs=[
            pl.BlockSpec((1, 2, tm), lambda i: (i, 0, 0), memory_space=pltpu.SMEM),
            pl.BlockSpec((tm, LANES), lambda i: (i, 0)),
            pl.BlockSpec((tm, d), lambda i: (i, 0)),
            pl.BlockSpec((1, 1, d), lambda i: ((layer * batch + i // per_b) * 6 + 5, 0, 0)),
            pl.BlockSpec(memory_space=pl.ANY),
        ],
        out_specs=pl.BlockSpec((tm, d), lambda i: (i, 0)),
        scratch_shapes=[pltpu.VMEM((2, tm, d), F32), pltpu.SemaphoreType.DMA(())],
        compiler_params=_cparams(1, 32),
        name="moe_combine",
    )(pos_t, gates, x1, mod, ys)


def _moe_layer(h2p, logits, x1, mod, layer, batch, w_gu, w_down):
    n = x1.shape[0]
    tm = min(512, n)
    tr = min(512, n)
    route, gates, counts = _route(logits)
    counts = counts[0, :N_EXPERTS]
    padded = ((counts + tm - 1) // tm) * tm
    ends = jnp.cumsum(padded)
    offs = (ends - padded).astype(I32)
    n_tiles = (2 * n) // tm + N_EXPERTS
    rows_padded = n_tiles * tm
    tile_start = jnp.arange(n_tiles, dtype=I32) * tm
    tile_valid = (tile_start < ends[-1]).astype(I32)
    tile_expert = jnp.minimum(jnp.sum(tile_start[:, None] >= ends[None, :], axis=1), N_EXPERTS - 1).astype(I32)
    tile_expert = tile_expert + (layer // 2) * N_EXPERTS
    pos = offs[route[:, 0:2]] + route[:, 2:4]
    pos_t = pos.reshape(n // tr, tr, 2).transpose(0, 2, 1)
    xs = _dispatch(h2p, pos_t, rows_padded, tr)
    ys = _expert_ffn(xs, tile_expert, tile_valid, w_gu, w_down, tm)
    return _combine(ys, pos_t, gates, x1, mod, layer, batch, tr)


def kernel(x, c, positions, norm1_g, norm2_g, ada_w, ada_b, pool_w, pool_scale, attn_w_qkv, attn_w_o,
           q_norm_g, k_norm_g, ffn_w_gu, ffn_w_down, router_w, router_b, moe_w_gu, moe_w_down):
    batch, seq, d = x.shape
    depth = ada_w.shape[0]
    n = batch * seq
    mod = _ada_modulation(c, ada_w, ada_b).reshape(depth * batch * 6, 1, d)
    cs, sn = _rope_tables(positions)
    pool_wb, ffn_gub, ffn_db = pool_w.astype(BF16), ffn_w_gu.astype(BF16), ffn_w_down.astype(BF16)
    qkv_wb, wo_b = attn_w_qkv.astype(BF16), attn_w_o.astype(BF16)
    moe_gub = moe_w_gu.astype(BF16).reshape((-1,) + moe_w_gu.shape[2:])
    moe_db = moe_w_down.astype(BF16).reshape((-1,) + moe_w_down.shape[2:])
    xf = x.reshape(n, d)
    for layer in range(depth):
        j = layer // 2
        if layer % 2 == 0:
            x1, h2 = _pool_layer(xf.reshape(batch, seq, d), mod, layer, norm1_g[layer], norm2_g[layer],
                                 pool_wb, pool_scale[j])
            xf = _ffn_layer(h2.reshape(n, d), x1.reshape(n, d), mod, layer, batch, ffn_gub, ffn_db)
        else:
            qkvs = _qkv_layer(xf.reshape(batch, seq, d), mod, layer, norm1_g[layer],
                              qkv_wb, q_norm_g[j], k_norm_g[j], cs, sn)
            outs = [_attn_group(qkv_g, g) for g, qkv_g in enumerate(qkvs)]
            x1, h2p, logits = _merge_layer([o for o, _ in outs], [l for _, l in outs], xf, mod, layer, batch,
                                           wo_b, norm2_g[layer], router_w[j], router_b[j])
            xf = _moe_layer(h2p, logits, x1, mod, layer, batch, moe_gub, moe_db)
    return xf.reshape(batch, seq, d)
```
